```python
import math
import jax, jax.numpy as jnp
from jax import lax
import numpy as np

D_MODEL = 4096
BATCH = 4
SEQ = 2048
DEPTH = 1
DEC_BATCH = 128
DEC_SEQ = 8
PAST_LEN = 8192
PAGE_SIZE = 128

M_HEADS = 8
M_WIDTH = D_MODEL // 2
M_HEAD_DIM = M_WIDTH // M_HEADS
M_CHUNK = 64
A_HEAD_DIM = 64
A_WIDTH = D_MODEL // 2
A_HEADS = A_WIDTH // A_HEAD_DIM
A_KV_HEADS = A_HEADS // 4
A_GROUP = A_HEADS // A_KV_HEADS
A_KV_WIDTH = A_KV_HEADS * A_HEAD_DIM
WINDOW = 128
ROPE_THETA = 10000.0
LN_EPS = 1e-5
DEEPNORM_ALPHA = (2.0 * DEPTH) ** 0.25
DEEPNORM_BETA = (8.0 * DEPTH) ** -0.25

IN_COLUMNS = (('q_m', M_WIDTH), ('k_m', M_WIDTH), ('v_m', M_WIDTH), ('o_m', M_WIDTH), ('z_m', M_WIDTH),
              ('i_m', M_HEADS), ('f_m', M_HEADS),
              ('q_a', A_WIDTH), ('k_a', A_KV_WIDTH), ('v_a', A_KV_WIDTH), ('z_a', A_WIDTH),
              ('g_m', D_MODEL), ('g_a', D_MODEL))
IN_WIDTH = sum(size for _, size in IN_COLUMNS)
SPLIT_POINTS = tuple(int(p) for p in np.cumsum([size for _, size in IN_COLUMNS])[:-1])

kernel_name = 'hybrid_mlstm_swa_sink_decode_step'


def _layer_norm(x, g, b):
    xf = x.astype(jnp.float32)
    mu = xf.mean(-1, keepdims=True)
    var = jnp.mean(jnp.square(xf - mu), -1, keepdims=True)
    return ((xf - mu) * lax.rsqrt(var + LN_EPS) * g + b).astype(x.dtype)


def _rope(x, pos):
    half = x.shape[-1] // 2
    inv = ROPE_THETA ** (-jnp.arange(half, dtype=jnp.float32) / half)
    ang = pos.astype(jnp.float32)[:, None] * inv[None, :]
    cos = jnp.cos(ang)[None, :, None, :]
    sin = jnp.sin(ang)[None, :, None, :]
    xf = x.astype(jnp.float32)
    x1, x2 = xf[..., :half], xf[..., half:]
    return jnp.concatenate([x1 * cos - x2 * sin, x2 * cos + x1 * sin], -1).astype(x.dtype)


def _mlstm_scan(q, k, v, ig, lf, C0, n0, m0):
    B, H, T, _ = q.shape
    L = math.gcd(T, M_CHUNK)
    nc = T // L
    causal = jnp.tril(jnp.ones((L, L), bool))

    def to_chunks(a):
        return jnp.moveaxis(a.reshape((B, H, nc, L) + a.shape[3:]), 2, 0)

    def step(carry, xs):
        C, n, m = carry
        qc, kc, vc, ic, fc = xs
        b = jnp.cumsum(fc, axis=-1)
        log_d = b[..., :, None] - b[..., None, :] + ic[..., None, :]
        log_d = jnp.where(causal, log_d, -jnp.inf)
        a = b + m[..., None]
        m_t = jnp.maximum(a, log_d.max(-1))
        s = jnp.einsum('bhtd,bhsd->bhts', qc, kc) * jnp.exp(log_d - m_t[..., None])
        inter = jnp.exp(a - m_t)
        num = jnp.einsum('bhts,bhse->bhte', s, vc) + inter[..., None] * jnp.einsum('bhtd,bhde->bhte', qc, C)
        den = s.sum(-1) + inter * jnp.einsum('bhtd,bhd->bht', qc, n)
        h = num / jnp.maximum(jnp.abs(den), jnp.exp(-m_t))[..., None]
        m_new = m_t[..., -1]
        w = jnp.exp(b[..., -1:] - b + ic - m_new[..., None])
        decay = jnp.exp(b[..., -1] + m - m_new)
        C_new = decay[..., None, None] * C + jnp.einsum('bhsd,bhse->bhde', w[..., None] * kc, vc)
        n_new = decay[..., None] * n + jnp.einsum('bhs,bhsd->bhd', w, kc)
        return (C_new, n_new, m_new), h

    init = (C0.astype(jnp.float32), n0.astype(jnp.float32), m0.astype(jnp.float32))
    (C, n, m), h = lax.scan(step, init, tuple(to_chunks(a) for a in (q, k, v, ig, lf)))
    h = jnp.moveaxis(h, 0, 2).reshape(B, H, T, v.shape[-1])
    return h, C, n, m


def _mlstm_branch(qm, km, vm, om, zm, im, fm, b_if, norm_m_g, state0):
    B, T, _ = qm.shape

    def heads(a):
        return a.reshape(B, T, M_HEADS, -1).transpose(0, 2, 1, 3).astype(jnp.float32)

    q = heads(qm)
    k = heads(km) * (M_HEAD_DIM ** -0.5)
    v = heads(vm)
    ig = (im.astype(jnp.float32) + b_if[:M_HEADS]).transpose(0, 2, 1)
    lf = jax.nn.log_sigmoid(fm.astype(jnp.float32) + b_if[M_HEADS:]).transpose(0, 2, 1)
    h, C, n, m = _mlstm_scan(q, k, v, ig, lf, *state0)
    h = jax.nn.sigmoid(om.astype(jnp.float32)).reshape(B, T, M_HEADS, M_HEAD_DIM) * h.transpose(0, 2, 1, 3)
    mu = h.mean(-1, keepdims=True)
    var = jnp.mean(jnp.square(h - mu), -1, keepdims=True)
    h = (h - mu) * lax.rsqrt(var + LN_EPS) * norm_m_g.reshape(M_HEADS, M_HEAD_DIM)
    out = h.reshape(B, T, M_WIDTH).astype(qm.dtype) * jax.nn.silu(zm)
    return out, C, n, m


def _sink_probs(s, mask, sink):
    s = jnp.where(mask, s, -jnp.inf)
    mx = jnp.maximum(s.max(-1, keepdims=True), sink)
    p = jnp.exp(s - mx)
    return p / (p.sum(-1, keepdims=True) + jnp.exp(sink - mx))


def _swa_prompt(q, k, v, sinks):
    B, S, H, hd = q.shape
    nb = S // WINDOW
    qb = q.reshape(B, nb, WINDOW, A_KV_HEADS, A_GROUP, hd)
    kb = k.reshape(B, nb, WINDOW, A_KV_HEADS, hd)
    vb = v.reshape(B, nb, WINDOW, A_KV_HEADS, hd)

    def with_prev(a):
        prev = jnp.pad(a[:, :-1], ((0, 0), (1, 0), (0, 0), (0, 0), (0, 0)))
        return jnp.concatenate([prev, a], axis=2)

    kk, vv = with_prev(kb), with_prev(vb)
    s = jnp.einsum('bnqkgd,bnskd->bnkgqs', qb, kk).astype(jnp.float32) * (hd ** -0.5)
    blk = jnp.arange(nb)[:, None, None] * WINDOW
    qpos = blk + jnp.arange(WINDOW)[None, :, None]
    kpos = blk - WINDOW + jnp.arange(2 * WINDOW)[None, None, :]
    rel = qpos - kpos
    mask = (rel >= 0) & (rel < WINDOW) & (kpos >= 0)
    sink = sinks.reshape(A_KV_HEADS, A_GROUP)[:, :, None, None].astype(jnp.float32)
    p = _sink_probs(s, mask[None, :, None, None], sink)
    o = jnp.einsum('bnkgqs,bnskd->bnqkgd', p.astype(vv.dtype), vv)
    return o.reshape(B, S, H * hd)


def _swa_sample(q, k, v, cache_k, cache_v, sinks):
    B, T, H, hd = q.shape
    w_buf = cache_k.shape[1]
    kk = jnp.concatenate([cache_k.astype(k.dtype), k], axis=1)
    vv = jnp.concatenate([cache_v.astype(v.dtype), v], axis=1)
    qg = q.reshape(B, T, A_KV_HEADS, A_GROUP, hd)
    s = jnp.einsum('btkgd,bskd->bkgts', qg, kk).astype(jnp.float32) * (hd ** -0.5)
    qpos = PAST_LEN + jnp.arange(T)
    kpos = PAST_LEN - w_buf + jnp.arange(w_buf + T)
    rel = qpos[:, None] - kpos[None, :]
    mask = (rel >= 0) & (rel < WINDOW)
    sink = sinks.reshape(A_KV_HEADS, A_GROUP)[:, :, None, None].astype(jnp.float32)
    p = _sink_probs(s, mask, sink)
    o = jnp.einsum('bkgts,bskd->btkgd', p.astype(vv.dtype), vv).reshape(B, T, H * hd)
    return o, kk[:, -w_buf:], vv[:, -w_buf:]


def _hybrid_layer(x, positions, state0, attend, w_in, b_if, norm_m_g, w_bm, w_ba, w_out, ln_g, ln_b):
    B, T, _ = x.shape
    qm, km, vm, om, zm, im, fm, qa, ka, va, za, gm, ga = jnp.split(x @ w_in, SPLIT_POINTS, axis=-1)
    mix_m, C, n, m = _mlstm_branch(qm, km, vm, om, zm, im, fm, b_if, norm_m_g, state0)
    qa = _rope(qa.reshape(B, T, A_HEADS, A_HEAD_DIM), positions)
    ka = _rope(ka.reshape(B, T, A_KV_HEADS, A_HEAD_DIM), positions)
    va = va.reshape(B, T, A_KV_HEADS, A_HEAD_DIM)
    att, k_buf, v_buf = attend(qa, ka, va)
    mix_a = att * jax.nn.silu(za)
    merged = jax.nn.sigmoid(gm) * (mix_m @ w_bm) + jax.nn.sigmoid(ga) * (mix_a @ w_ba)
    y = _layer_norm(DEEPNORM_ALPHA * x + merged @ w_out, ln_g, ln_b)
    return y, C, n, m, k_buf, v_buf


def setup_inputs(seed: int = 0) -> dict:
    key = jax.random.key(seed)
    ks = jax.random.split(key, 18)
    f32 = jnp.float32
    w_buf = min(WINDOW, PAST_LEN)

    def nrm(k, shape):
        return jax.random.normal(k, shape, f32)

    col_scale = jnp.concatenate([jnp.full((size,), DEEPNORM_BETA if name in ('v_m', 'v_a') else 1.0, f32)
                                 for name, size in IN_COLUMNS])
    return {
        'x_prompt': nrm(ks[0], (BATCH, SEQ, D_MODEL)),
        'x_sample': nrm(ks[1], (DEC_BATCH, DEC_SEQ, D_MODEL)),
        'state_C': 0.3 * nrm(ks[2], (DEPTH, DEC_BATCH, M_HEADS, M_HEAD_DIM, M_HEAD_DIM)),
        'state_n': 0.1 * nrm(ks[3], (DEPTH, DEC_BATCH, M_HEADS, M_HEAD_DIM)),
        'state_m': 0.5 * nrm(ks[4], (DEPTH, DEC_BATCH, M_HEADS)),
        'cache_k': nrm(ks[5], (DEPTH, DEC_BATCH, w_buf, A_KV_HEADS, A_HEAD_DIM)),
        'cache_v': nrm(ks[6], (DEPTH, DEC_BATCH, w_buf, A_KV_HEADS, A_HEAD_DIM)),
        'w_in': nrm(ks[7], (DEPTH, D_MODEL, IN_WIDTH)) * (D_MODEL ** -0.5) * col_scale,
        'b_if': jnp.concatenate([0.1 * nrm(ks[8], (DEPTH, M_HEADS)),
                                 3.0 + 3.0 * jax.random.uniform(ks[9], (DEPTH, M_HEADS), f32)], axis=-1),
        'norm_m_g': 1.0 + 0.02 * nrm(ks[10], (DEPTH, M_WIDTH)),
        'attn_sinks': 0.5 * nrm(ks[11], (DEPTH, A_HEADS)),
        'w_bm': nrm(ks[12], (DEPTH, M_WIDTH, D_MODEL)) * (M_WIDTH ** -0.5 * DEEPNORM_BETA),
        'w_ba': nrm(ks[13], (DEPTH, A_WIDTH, D_MODEL)) * (A_WIDTH ** -0.5 * DEEPNORM_BETA),
        'w_out': nrm(ks[14], (DEPTH, D_MODEL, D_MODEL)) * (D_MODEL ** -0.5 * DEEPNORM_BETA),
        'ln_g': 1.0 + 0.02 * nrm(ks[15], (DEPTH, D_MODEL)),
        'ln_b': 0.02 * nrm(ks[16], (DEPTH, D_MODEL)),
    }


def reference(x_prompt, x_sample, state_C, state_n, state_m, cache_k, cache_v, w_in, b_if, norm_m_g,
              attn_sinks, w_bm, w_ba, w_out, ln_g, ln_b):
    bp, seq = x_prompt.shape[0], x_prompt.shape[1]
    w_buf = cache_k.shape[2]
    pos_prompt = jnp.arange(seq)
    pos_sample = PAST_LEN + jnp.arange(x_sample.shape[1])
    fresh_state = (jnp.zeros((bp, M_HEADS, M_HEAD_DIM, M_HEAD_DIM), jnp.float32),
                   jnp.zeros((bp, M_HEADS, M_HEAD_DIM), jnp.float32),
                   jnp.zeros((bp, M_HEADS), jnp.float32))
    y_p, y_s = x_prompt, x_sample
    outs_p, outs_s = [], []
    for l in range(DEPTH):
        weights = (w_in[l], b_if[l], norm_m_g[l], w_bm[l], w_ba[l], w_out[l], ln_g[l], ln_b[l])
        sinks = attn_sinks[l]
        ck, cv = cache_k[l], cache_v[l]

        def attend_prompt(q, k, v):
            return _swa_prompt(q, k, v, sinks), k[:, seq - w_buf:], v[:, seq - w_buf:]

        def attend_sample(q, k, v):
            return _swa_sample(q, k, v, ck, cv, sinks)

        y_p, *st_p = _hybrid_layer(y_p, pos_prompt, fresh_state, attend_prompt, *weights)
        y_s, *st_s = _hybrid_layer(y_s, pos_sample, (state_C[l], state_n[l], state_m[l]), attend_sample, *weights)
        outs_p.append(st_p)
        outs_s.append(st_s)

    def stack(outs, i, like):
        return jnp.stack([o[i] for o in outs]).astype(like.dtype)

    return (y_p, y_s,
            stack(outs_p, 0, state_C), stack(outs_p, 1, state_n), stack(outs_p, 2, state_m),
            stack(outs_p, 3, cache_k), stack(outs_p, 4, cache_v),
            stack(outs_s, 0, state_C), stack(outs_s, 1, state_n), stack(outs_s, 2, state_m),
            stack(outs_s, 3, cache_k), stack(outs_s, 4, cache_v))
```

```python
import functools

import jax
import jax.numpy as jnp
from jax import lax
from jax.experimental import pallas as pl
from jax.experimental.pallas import tpu as pltpu

F32 = jnp.float32
BF16 = jnp.bfloat16

D_MODEL = 4096
DEPTH = 1
PAST_LEN = 8192
M_HEADS = 8
M_WIDTH = D_MODEL // 2
M_HEAD_DIM = M_WIDTH // M_HEADS
A_HEAD_DIM = 64
A_WIDTH = D_MODEL // 2
A_HEADS = A_WIDTH // A_HEAD_DIM
A_KV_HEADS = A_HEADS // 4
A_GROUP = A_HEADS // A_KV_HEADS
A_KV_WIDTH = A_KV_HEADS * A_HEAD_DIM
WINDOW = 128
ROPE_THETA = 10000.0
LN_EPS = 1e-5
DEEPNORM_ALPHA = (2.0 * DEPTH) ** 0.25

LANES = 128
VMEM_LIMIT = 56 * 1024 * 1024

_OFF_IF = 5 * M_WIDTH
_OFF_QA = _OFF_IF + 2 * M_HEADS
_OFF_KA = _OFF_QA + A_WIDTH
_OFF_VA = _OFF_KA + A_KV_WIDTH
_OFF_ZA = _OFF_VA + A_KV_WIDTH
_OFF_G = _OFF_ZA + A_WIDTH

PROJ_TM = 1024
PROJ_TN = 1024
MLSTM_COLS = 5 * M_WIDTH
ATTN_COLS = 2 * A_WIDTH + 2 * A_KV_WIDTH
GATE_COLS = 2 * D_MODEL


def _params(sem):
    return pltpu.CompilerParams(dimension_semantics=sem, vmem_limit_bytes=VMEM_LIMIT)


def _rope_slab(slab, cos, sin, lo):
    partner = jnp.where(lo, pltpu.roll(slab, LANES - 32, 1), pltpu.roll(slab, 32, 1))
    return slab * cos + partner * sin


def _proj_body(x_ref, w_ref, *rest, mode):
    acc = jnp.dot(x_ref[...], w_ref[...], preferred_element_type=F32)
    if mode == "plain":
        (o_ref,) = rest
        o_ref[...] = acc.astype(o_ref.dtype)
    elif mode == "sigmoid":
        (o_ref,) = rest
        o_ref[...] = jax.nn.sigmoid(acc).astype(o_ref.dtype)
    else:
        cos_ref, sin_ref, o_ref = rest
        tm, tn = acc.shape
        j = pl.program_id(0)
        n_q = A_WIDTH // tn
        n_plain = n_q + A_WIDTH // tn
        rope_w = A_KV_WIDTH

        def store(width):
            cos = cos_ref[...]
            sin = sin_ref[...]
            lo = (lax.broadcasted_iota(jnp.int32, (tm, LANES), 1) % A_HEAD_DIM) < (A_HEAD_DIM // 2)
            for c in range(tn // LANES):
                slab = acc[:, c * LANES:(c + 1) * LANES]
                if c * LANES < width:
                    slab = _rope_slab(slab, cos, sin, lo)
                o_ref[:, c * LANES:(c + 1) * LANES] = slab.astype(o_ref.dtype)

        @pl.when(j < n_q)
        def _():
            store(tn)

        @pl.when((j >= n_q) & (j < n_plain))
        def _():
            store(0)

        @pl.when(j >= n_plain)
        def _():
            store(rope_w)


def _project(xb, w, col_block0, n_blocks, *, tn, out_dtype, mode, cos=None, sin=None, name):
    m_rows = xb.shape[0]
    tm = min(PROJ_TM, m_rows)
    in_specs = [pl.BlockSpec((tm, D_MODEL), lambda j, i: (i, 0)),
                pl.BlockSpec((D_MODEL, tn), lambda j, i: (0, j + col_block0))]
    args = [xb, w]
    if mode == "attn":
        in_specs += [pl.BlockSpec((tm, LANES), lambda j, i: (i, 0))] * 2
        args += [cos, sin]
    return pl.pallas_call(
        functools.partial(_proj_body, mode=mode),
        grid=(n_blocks, m_rows // tm),
        in_specs=in_specs,
        out_specs=pl.BlockSpec((tm, tn), lambda j, i: (i, j)),
        out_shape=jax.ShapeDtypeStruct((m_rows, n_blocks * tn), out_dtype),
        compiler_params=_params(("parallel", "parallel")),
        name=name,
    )(*args)


def _log_sigmoid(x):
    return jnp.minimum(x, 0.0) - jnp.log(1.0 + jnp.exp(-jnp.abs(x)))


def _cumsum_rows(a):
    n = a.shape[0]
    row = lax.broadcasted_iota(jnp.int32, a.shape, 0)
    shift = 1
    while shift < n:
        a = a + jnp.where(row >= shift, pltpu.roll(a, shift, 0), 0.0)
        shift *= 2
    return a


def _mlstm_body(*refs, chunk, n_seq, has_init):
    q_ref, k_ref, v_ref, o_ref, z_ref, g_ref, bif_ref, ng_ref = refs[:8]
    if has_init:
        c0_ref, n0_ref, m0_ref = refs[8:11]
        mix_ref, c_ref, n_ref, m_ref = refs[11:]
    else:
        mix_ref, c_ref, n_ref, m_ref = refs[8:]
        c0_ref, n0_ref, m0_ref = c_ref, n_ref, m_ref

        @pl.when(pl.program_id(1) == 0)
        def _():
            c_ref[...] = jnp.zeros_like(c_ref)
            n_ref[...] = jnp.zeros_like(n_ref)
            m_ref[...] = jnp.zeros_like(m_ref)

    L = chunk
    dh = M_HEAD_DIM
    lane = lax.broadcasted_iota(jnp.int32, (L, LANES), 1)
    causal = (lax.broadcasted_iota(jnp.int32, (L, L), 1) <= lax.broadcasted_iota(jnp.int32, (L, L), 0))
    pad_rows = (-L) % LANES

    gate_cols, gate_rows = [], []
    for s in range(n_seq):
        g = g_ref[s * L:(s + 1) * L, :] + bif_ref[...]
        b = _cumsum_rows(jnp.where(lane >= M_HEADS, _log_sigmoid(g), 0.0))
        col = jnp.where(lane < M_HEADS, g, b)
        colp = jnp.concatenate([col, jnp.zeros((pad_rows, LANES), F32)], axis=0) if pad_rows else col
        gate_cols.append(col)
        gate_rows.append(colp.T)

    for h in range(M_HEADS):
        cols = slice(h * dh, (h + 1) * dh)
        mix_parts = []
        for s in range(n_seq):
            rows = slice(s * L, (s + 1) * L)
            q = q_ref[rows, cols]
            k = k_ref[rows, cols] * (dh ** -0.5)
            v = v_ref[rows, cols]
            qb, kb, vb = q.astype(BF16), k.astype(BF16), v.astype(BF16)
            i_c = gate_cols[s][:, h:h + 1]
            b_c = gate_cols[s][:, M_HEADS + h:M_HEADS + h + 1]
            i_r = gate_rows[s][h:h + 1, :L]
            b_r = gate_rows[s][M_HEADS + h:M_HEADS + h + 1, :L]
            m_prev = m0_ref[s, h:h + 1, 0:1]
            c_prev = c0_ref[s, h]
            n_prev = n0_ref[s, h:h + 1, :]

            log_d = jnp.where(causal, (b_c - b_r) + i_r, -jnp.inf)
            a_c = b_c + m_prev
            m_t = jnp.maximum(a_c, jnp.max(log_d, axis=1, keepdims=True))
            sc = lax.dot_general(qb, kb, (((1,), (1,)), ((), ())), preferred_element_type=F32)
            sc = sc * jnp.exp(log_d - m_t)
            inter = jnp.exp(a_c - m_t)
            num = (jnp.dot(sc.astype(BF16), vb, preferred_element_type=F32)
                   + inter * jnp.dot(qb, c_prev.astype(BF16), preferred_element_type=F32))
            qn = jnp.sum(q.astype(F32) * n_prev, axis=1, keepdims=True)
            den = jnp.sum(sc, axis=1, keepdims=True) + inter * qn
            hid = num / jnp.maximum(jnp.abs(den), jnp.exp(-m_t))

            m_new = m_t[L - 1:L, :]
            b_last = b_c[L - 1:L, :]
            w_c = jnp.exp((b_last - b_c) + i_c - m_new)
            decay = jnp.exp(b_last + m_prev - m_new)
            kw = k.astype(F32) * w_c
            c_ref[s, h] = decay * c_prev + lax.dot_general(
                kw.astype(BF16), vb, (((0,), (0,)), ((), ())), preferred_element_type=F32)
            n_ref[s, h:h + 1, :] = decay * n_prev + jnp.sum(kw, axis=0, keepdims=True)
            m_ref[s, h:h + 1, :] = jnp.broadcast_to(m_new, (1, LANES))

            hid = jax.nn.sigmoid(o_ref[rows, cols].astype(F32)) * hid
            mu = jnp.mean(hid, axis=1, keepdims=True)
            var = jnp.mean(jnp.square(hid - mu), axis=1, keepdims=True)
            hid = (hid - mu) * lax.rsqrt(var + LN_EPS) * ng_ref[:, cols]
            mix_parts.append(hid * jax.nn.silu(z_ref[rows, cols].astype(F32)))
        mix = mix_parts[0] if n_seq == 1 else jnp.concatenate(mix_parts, axis=0)
        mix_ref[:, cols] = mix.astype(mix_ref.dtype)


def _mlstm(pm, gates, bif, ng, *, n_batch, seq, chunk, n_seq, init=None, name):
    rows = n_seq * chunk
    n_chunks = seq // chunk
    nw = M_WIDTH // M_WIDTH
    del nw

    def grp(gidx):
        return pl.BlockSpec((rows, M_WIDTH), lambda b, c: (b * n_chunks + c, gidx))

    in_specs = [grp(0), grp(1), grp(2), grp(3), grp(4),
                pl.BlockSpec((rows, LANES), lambda b, c: (b * n_chunks + c, 0)),
                pl.BlockSpec((1, LANES), lambda b, c: (0, 0)),
                pl.BlockSpec((1, M_WIDTH), lambda b, c: (0, 0))]
    args = [pm, pm, pm, pm, pm, gates, bif, ng]
    st_specs = [pl.BlockSpec((n_seq, M_HEADS, M_HEAD_DIM, M_HEAD_DIM), lambda b, c: (b, 0, 0, 0)),
                pl.BlockSpec((n_seq, M_HEADS, M_HEAD_DIM), lambda b, c: (b, 0, 0)),
                pl.BlockSpec((n_seq, M_HEADS, LANES), lambda b, c: (b, 0, 0))]
    if init is not None:
        in_specs += st_specs
        args += list(init)
    n_grid = n_batch // n_seq
    return pl.pallas_call(
        functools.partial(_mlstm_body, chunk=chunk, n_seq=n_seq, has_init=init is not None),
        grid=(n_grid, n_chunks),
        in_specs=in_specs,
        out_specs=[pl.BlockSpec((rows, M_WIDTH), lambda b, c: (b * n_chunks + c, 0))] + st_specs,
        out_shape=[jax.ShapeDtypeStruct((n_batch * seq, M_WIDTH), BF16),
                   jax.ShapeDtypeStruct((n_batch, M_HEADS, M_HEAD_DIM, M_HEAD_DIM), F32),
                   jax.ShapeDtypeStruct((n_batch, M_HEADS, M_HEAD_DIM), F32),
                   jax.ShapeDtypeStruct((n_batch, M_HEADS, LANES), F32)],
        compiler_params=_params(("parallel", "arbitrary")),
        name=name,
    )(*args)


def _sink_softmax_pv(scores, values, sink):
    mx = sink
    for sc in scores:
        mx = jnp.maximum(mx, jnp.max(sc, axis=1, keepdims=True))
    den = jnp.exp(sink - mx)
    acc = None
    for sc, vv in zip(scores, values):
        p = jnp.exp(sc - mx)
        den = den + jnp.sum(p, axis=1, keepdims=True)
        pv = jnp.dot(p.astype(BF16), vv, preferred_element_type=F32)
        acc = pv if acc is None else acc + pv
    return acc / den


def _swa_prompt_body(sink_ref, q_ref, z_ref, kc_ref, kp_ref, vc_ref, vp_ref, o_ref):
    W = WINDOW
    hd = A_HEAD_DIM
    has_prev = pl.program_id(1) > 0
    t_idx = lax.broadcasted_iota(jnp.int32, (A_GROUP * W, W), 0) % W
    j_idx = lax.broadcasted_iota(jnp.int32, (A_GROUP * W, W), 1)
    mask_c = j_idx <= t_idx
    mask_p = (j_idx > t_idx) & has_prev
    nt = (((1,), (1,)), ((), ()))
    outs = []
    for kh in range(A_KV_HEADS):
        heads = [kh * A_GROUP + g for g in range(A_GROUP)]
        q4 = jnp.concatenate([q_ref[:, h * hd:(h + 1) * hd] for h in heads], axis=0) * (hd ** -0.5)
        sink = jnp.concatenate([jnp.full((W, 1), sink_ref[h], F32) for h in heads], axis=0)
        kv = slice(kh * hd, (kh + 1) * hd)
        s_c = jnp.where(mask_c, lax.dot_general(q4, kc_ref[:, kv], nt, preferred_element_type=F32), -jnp.inf)
        s_p = jnp.where(mask_p, lax.dot_general(q4, kp_ref[:, kv], nt, preferred_element_type=F32), -jnp.inf)
        o4 = _sink_softmax_pv([s_p, s_c], [vp_ref[:, kv], vc_ref[:, kv]], sink)
        outs += [o4[g * W:(g + 1) * W, :] for g in range(A_GROUP)]
    att = jnp.concatenate(outs, axis=1)
    o_ref[...] = (att * jax.nn.silu(z_ref[...].astype(F32))).astype(o_ref.dtype)


def _swa_prompt(pa, sinks, *, n_batch, seq, name):
    nb = seq // WINDOW
    kblk = A_WIDTH // A_KV_WIDTH

    def cur(col):
        return lambda b, n: (b * nb + n, col)

    def prev(col):
        return lambda b, n: (b * nb + jnp.maximum(n - 1, 0), col)

    return pl.pallas_call(
        _swa_prompt_body,
        grid=(n_batch, nb),
        in_specs=[pl.BlockSpec(memory_space=pltpu.SMEM),
                  pl.BlockSpec((WINDOW, A_WIDTH), cur(0)),
                  pl.BlockSpec((WINDOW, A_WIDTH), cur(1)),
                  pl.BlockSpec((WINDOW, A_KV_WIDTH), cur(2 * kblk)),
                  pl.BlockSpec((WINDOW, A_KV_WIDTH), prev(2 * kblk)),
                  pl.BlockSpec((WINDOW, A_KV_WIDTH), cur(2 * kblk + 1)),
                  pl.BlockSpec((WINDOW, A_KV_WIDTH), prev(2 * kblk + 1))],
        out_specs=pl.BlockSpec((WINDOW, A_WIDTH), cur(0)),
        out_shape=jax.ShapeDtypeStruct((n_batch * seq, A_WIDTH), BF16),
        compiler_params=_params(("parallel", "parallel")),
        name=name,
    )(sinks, pa, pa, pa, pa, pa, pa)


def _swa_sample_body(sink_ref, q_ref, z_ref, kn_ref, vn_ref, ck_ref, cv_ref, o_ref, ko_ref, vo_ref, *, n_seq, t_new):
    W = WINDOW
    hd = A_HEAD_DIM
    T = t_new
    R = A_GROUP * T
    S = 2 * W
    t_idx = lax.broadcasted_iota(jnp.int32, (R, S), 0) % T
    j_idx = lax.broadcasted_iota(jnp.int32, (R, S), 1)
    mask = ((j_idx < W) & (j_idx > t_idx)) | ((j_idx >= W) & (j_idx - W <= t_idx))
    nt = (((1,), (1,)), ((), ()))
    pad = jnp.zeros((S - W - T, A_KV_WIDTH), F32)
    att_rows = []
    for s in range(n_seq):
        rows = slice(s * T, (s + 1) * T)
        kk = jnp.concatenate([ck_ref[s], kn_ref[rows, :], pad], axis=0)
        vv = jnp.concatenate([cv_ref[s], vn_ref[rows, :], pad], axis=0)
        ko_ref[s] = kk[T:T + W, :]
        vo_ref[s] = vv[T:T + W, :]
        kkb, vvb = kk.astype(BF16), vv.astype(BF16)
        outs = []
        for kh in range(A_KV_HEADS):
            heads = [kh * A_GROUP + g for g in range(A_GROUP)]
            q4 = jnp.concatenate([q_ref[rows, h * hd:(h + 1) * hd] for h in heads], axis=0) * (hd ** -0.5)
            sink = jnp.concatenate([jnp.full((T, 1), sink_ref[h], F32) for h in heads], axis=0)
            kv = slice(kh * hd, (kh + 1) * hd)
            sc = lax.dot_general(q4.astype(BF16), kkb[:, kv], nt, preferred_element_type=F32)
            o4 = _sink_softmax_pv([jnp.where(mask, sc, -jnp.inf)], [vvb[:, kv]], sink)
            outs += [o4[g * T:(g + 1) * T, :] for g in range(A_GROUP)]
        att_rows.append(jnp.concatenate(outs, axis=1))
    att = att_rows[0] if n_seq == 1 else jnp.concatenate(att_rows, axis=0)
    o_ref[...] = (att * jax.nn.silu(z_ref[...].astype(F32))).astype(o_ref.dtype)


def _swa_sample(pa, cache_k, cache_v, sinks, *, n_batch, t_new, n_seq, name):
    rows = n_seq * t_new
    kblk = A_WIDTH // A_KV_WIDTH
    cache_spec = pl.BlockSpec((n_seq, WINDOW, A_KV_WIDTH), lambda b: (b, 0, 0))
    return pl.pallas_call(
        functools.partial(_swa_sample_body, n_seq=n_seq, t_new=t_new),
        grid=(n_batch // n_seq,),
        in_specs=[pl.BlockSpec(memory_space=pltpu.SMEM),
                  pl.BlockSpec((rows, A_WIDTH), lambda b: (b, 0)),
                  pl.BlockSpec((rows, A_WIDTH), lambda b: (b, 1)),
                  pl.BlockSpec((rows, A_KV_WIDTH), lambda b: (b, 2 * kblk)),
                  pl.BlockSpec((rows, A_KV_WIDTH), lambda b: (b, 2 * kblk + 1)),
                  cache_spec, cache_spec],
        out_specs=[pl.BlockSpec((rows, A_WIDTH), lambda b: (b, 0)), cache_spec, cache_spec],
        out_shape=[jax.ShapeDtypeStruct((n_batch * t_new, A_WIDTH), BF16),
                   jax.ShapeDtypeStruct(cache_k.shape, F32),
                   jax.ShapeDtypeStruct(cache_v.shape, F32)],
        compiler_params=_params(("parallel",)),
        name=name,
    )(sinks, pa, pa, pa, pa, cache_k, cache_v)


def _merge_body(mm_ref, ma_ref, wbm_ref, wba_ref, gm_ref, ga_ref, o_ref):
    bm = jnp.dot(mm_ref[...], wbm_ref[...], preferred_element_type=F32)
    ba = jnp.dot(ma_ref[...], wba_ref[...], preferred_element_type=F32)
    o_ref[...] = (gm_ref[...].astype(F32) * bm + ga_ref[...].astype(F32) * ba).astype(o_ref.dtype)


def _merge(mix_m, mix_a, w_bm, w_ba, gates, *, tn, name):
    m_rows = mix_m.shape[0]
    tm = min(1024, m_rows)
    nj = D_MODEL // tn
    return pl.pallas_call(
        _merge_body,
        grid=(m_rows // tm, nj),
        in_specs=[pl.BlockSpec((tm, M_WIDTH), lambda i, j: (i, 0)),
                  pl.BlockSpec((tm, A_WIDTH), lambda i, j: (i, 0)),
                  pl.BlockSpec((M_WIDTH, tn), lambda i, j: (0, j)),
                  pl.BlockSpec((A_WIDTH, tn), lambda i, j: (0, j)),
                  pl.BlockSpec((tm, tn), lambda i, j: (i, j)),
                  pl.BlockSpec((tm, tn), lambda i, j: (i, j + nj))],
        out_specs=pl.BlockSpec((tm, tn), lambda i, j: (i, j)),
        out_shape=jax.ShapeDtypeStruct((m_rows, D_MODEL), BF16),
        compiler_params=_params(("parallel", "parallel")),
        name=name,
    )(mix_m, mix_a, w_bm, w_ba, gates, gates)


def _out_body(mg_ref, w_ref, x_ref, g_ref, b_ref, o_ref, acc_ref, *, tn, row_chunk):
    j = pl.program_id(1)
    nj = acc_ref.shape[0]
    acc_ref[j] = DEEPNORM_ALPHA * x_ref[...] + jnp.dot(mg_ref[...], w_ref[...], preferred_element_type=F32)

    @pl.when(j == nj - 1)
    def _():
        def norm_rows(r, carry):
            rows = pl.ds(pl.multiple_of(r * row_chunk, row_chunk), row_chunk)
            parts = [acc_ref[jj, rows, :] for jj in range(nj)]
            mu = sum(jnp.sum(p, axis=1, keepdims=True) for p in parts) / D_MODEL
            var = sum(jnp.sum(jnp.square(p - mu), axis=1, keepdims=True) for p in parts) / D_MODEL
            inv = lax.rsqrt(var + LN_EPS)
            for jj, p in enumerate(parts):
                cols = slice(jj * tn, (jj + 1) * tn)
                o_ref[rows, cols] = (p - mu) * inv * g_ref[:, cols] + b_ref[:, cols]
            return carry

        lax.fori_loop(0, o_ref.shape[0] // row_chunk, norm_rows, 0)


def _out_proj(merged, w_out, x2d, ln_g, ln_b, *, tm, tn, name):
    m_rows = merged.shape[0]
    return pl.pallas_call(
        functools.partial(_out_body, tn=tn, row_chunk=16),
        scratch_shapes=[pltpu.VMEM((D_MODEL // tn, tm, tn), F32)],
        grid=(m_rows // tm, D_MODEL // tn),
        in_specs=[pl.BlockSpec((tm, D_MODEL), lambda i, j: (i, 0)),
                  pl.BlockSpec((D_MODEL, tn), lambda i, j: (0, j)),
                  pl.BlockSpec((tm, tn), lambda i, j: (i, j)),
                  pl.BlockSpec((1, D_MODEL), lambda i, j: (0, 0)),
                  pl.BlockSpec((1, D_MODEL), lambda i, j: (0, 0))],
        out_specs=pl.BlockSpec((tm, D_MODEL), lambda i, j: (i, 0)),
        out_shape=jax.ShapeDtypeStruct((m_rows, D_MODEL), F32),
        compiler_params=_params(("parallel", "arbitrary")),
        name=name,
    )(merged, w_out, x2d, ln_g, ln_b)


def _rope_tables(positions):
    half = A_HEAD_DIM // 2
    lane = jnp.arange(LANES)
    inv = ROPE_THETA ** (-(lane % half).astype(F32) / half)
    ang = positions.astype(F32)[:, None] * inv[None, :]
    sign = jnp.where((lane % A_HEAD_DIM) < half, -1.0, 1.0).astype(F32)
    return jnp.cos(ang), jnp.sin(ang) * sign[None, :]


def _layer(x2d, positions, wts, *, n_batch, seq, proj_dtype, tag):
    xb = x2d.astype(BF16)
    cos, sin = _rope_tables(positions)
    cos = jnp.tile(cos, (n_batch, 1))
    sin = jnp.tile(sin, (n_batch, 1))
    tn = PROJ_TN
    b0 = 0
    pm = _project(xb, wts["w_main"], b0, MLSTM_COLS // tn, tn=tn, out_dtype=proj_dtype, mode="plain",
                  name=f"proj_mlstm_{tag}")
    b0 += MLSTM_COLS // tn
    pa = _project(xb, wts["w_main"], b0, ATTN_COLS // tn, tn=tn, out_dtype=proj_dtype, mode="attn",
                  cos=cos, sin=sin, name=f"proj_attn_{tag}")
    b0 += ATTN_COLS // tn
    gates = _project(xb, wts["w_main"], b0, GATE_COLS // tn, tn=tn, out_dtype=BF16, mode="sigmoid",
                     name=f"proj_gate_{tag}")
    g_if = _project(xb, wts["w_if"], 0, 1, tn=LANES, out_dtype=F32, mode="plain", name=f"proj_if_{tag}")
    return xb, pm, pa, gates, g_if


def kernel(x_prompt, x_sample, state_C, state_n, state_m, cache_k, cache_v, w_in, b_if, norm_m_g,
           attn_sinks, w_bm, w_ba, w_out, ln_g, ln_b):
    bp, seq, _ = x_prompt.shape
    bs, t_new, _ = x_sample.shape
    w_buf = cache_k.shape[2]
    assert DEPTH == 1 and w_buf == WINDOW and seq % WINDOW == 0

    w = w_in[0]
    w_main = jnp.concatenate(
        [w[:, :_OFF_IF], w[:, _OFF_QA:_OFF_KA], w[:, _OFF_ZA:_OFF_G], w[:, _OFF_KA:_OFF_ZA], w[:, _OFF_G:]],
        axis=1).astype(BF16)
    w_if = jnp.pad(w[:, _OFF_IF:_OFF_QA], ((0, 0), (0, LANES - 2 * M_HEADS))).astype(BF16)
    wts = {"w_main": w_main, "w_if": w_if}
    bif = jnp.pad(b_if[0], (0, LANES - 2 * M_HEADS)).reshape(1, LANES)
    ng = norm_m_g[0].reshape(1, M_WIDTH)
    sinks = attn_sinks[0]
    wbm, wba, wo = w_bm[0].astype(BF16), w_ba[0].astype(BF16), w_out[0].astype(BF16)
    lng, lnb = ln_g[0].reshape(1, D_MODEL), ln_b[0].reshape(1, D_MODEL)

    xp2 = x_prompt.reshape(bp * seq, D_MODEL)
    _, pm, pa, gates, g_if = _layer(xp2, jnp.arange(seq), wts, n_batch=bp, seq=seq, proj_dtype=BF16, tag="p")
    mix_m, p_c, p_n, p_m = _mlstm(pm, g_if, bif, ng, n_batch=bp, seq=seq, chunk=128, n_seq=1, name="mlstm_p")
    mix_a = _swa_prompt(pa, sinks, n_batch=bp, seq=seq, name="swa_p")
    merged = _merge(mix_m, mix_a, wbm, wba, gates, tn=512, name="merge_p")
    y_p = _out_proj(merged, wo, xp2, lng, lnb, tm=512, tn=512, name="out_p").reshape(bp, seq, D_MODEL)
    kv_last = pa.reshape(bp, seq, ATTN_COLS)[:, seq - w_buf:, 2 * A_WIDTH:].astype(F32)
    p_k = kv_last[..., :A_KV_WIDTH].reshape(1, bp, w_buf, A_KV_HEADS, A_HEAD_DIM)
    p_v = kv_last[..., A_KV_WIDTH:].reshape(1, bp, w_buf, A_KV_HEADS, A_HEAD_DIM)

    xs2 = x_sample.reshape(bs * t_new, D_MODEL)
    _, pm, pa, gates, g_if = _layer(xs2, PAST_LEN + jnp.arange(t_new), wts, n_batch=bs, seq=t_new,
                                    proj_dtype=F32, tag="s")
    init = (state_C[0], state_n[0], jnp.broadcast_to(state_m[0][:, :, None], (bs, M_HEADS, LANES)))
    mix_m, s_c, s_n, s_m = _mlstm(pm, g_if, bif, ng, n_batch=bs, seq=t_new, chunk=t_new, n_seq=2, init=init,
                                  name="mlstm_s")
    mix_a, s_k, s_v = _swa_sample(pa, cache_k[0].reshape(bs, w_buf, A_KV_WIDTH),
                                  cache_v[0].reshape(bs, w_buf, A_KV_WIDTH), sinks,
                                  n_batch=bs, t_new=t_new, n_seq=2, name="swa_s")
    merged = _merge(mix_m, mix_a, wbm, wba, gates, tn=512, name="merge_s")
    y_s = _out_proj(merged, wo, xs2, lng, lnb, tm=512, tn=512, name="out_s").reshape(bs, t_new, D_MODEL)

    def st(a, like):
        return a[None].astype(like.dtype)

    return (y_p, y_s,
            st(p_c, state_C), st(p_n, state_n), st(p_m[:, :, 0], state_m), p_k, p_v,
            st(s_c, state_C), st(s_n, state_n), st(s_m[:, :, 0], state_m),
            s_k.reshape(1, bs, w_buf, A_KV_HEADS, A_HEAD_DIM), s_v.reshape(1, bs, w_buf, A_KV_HEADS, A_HEAD_DIM))
```

```python
import functools

import jax
import jax.numpy as jnp
from jax import lax
from jax.experimental import pallas as pl
from jax.experimental.pallas import tpu as pltpu

F32 = jnp.float32
BF16 = jnp.bfloat16

D_MODEL = 4096
DEPTH = 1
PAST_LEN = 8192
M_HEADS = 8
M_WIDTH = D_MODEL // 2
M_HEAD_DIM = M_WIDTH // M_HEADS
A_HEAD_DIM = 64
A_WIDTH = D_MODEL // 2
A_HEADS = A_WIDTH // A_HEAD_DIM
A_KV_HEADS = A_HEADS // 4
A_GROUP = A_HEADS // A_KV_HEADS
A_KV_WIDTH = A_KV_HEADS * A_HEAD_DIM
WINDOW = 128
ROPE_THETA = 10000.0
LN_EPS = 1e-5
DEEPNORM_ALPHA = (2.0 * DEPTH) ** 0.25

LANES = 128
SUBLANES = 8
VMEM_LIMIT = 56 * 1024 * 1024

_OFF_IF = 5 * M_WIDTH
_OFF_QA = _OFF_IF + 2 * M_HEADS

PROJ_TM = 1024
PROJ_TN = 1024
REPACK_TN = 512
REPACK_ROWS = 2048
MLSTM_COLS = 5 * M_WIDTH
ATTN_COLS = 2 * A_WIDTH + 2 * A_KV_WIDTH
GATE_COLS = 2 * D_MODEL
MAIN_COLS = MLSTM_COLS + ATTN_COLS + GATE_COLS


def _params(sem):
    return pltpu.CompilerParams(dimension_semantics=sem, vmem_limit_bytes=VMEM_LIMIT)


def _repack_src_block(t):
    n_main = MAIN_COLS // REPACK_TN
    za0 = (MLSTM_COLS + A_WIDTH) // REPACK_TN
    ka0 = za0 + A_WIDTH // REPACK_TN
    g0 = ka0 + 2 * A_KV_WIDTH // REPACK_TN
    kv_blocks = 2 * A_KV_WIDTH // REPACK_TN
    za_blocks = A_WIDTH // REPACK_TN
    moved = t + jnp.where((t >= za0) & (t < ka0), kv_blocks, 0) - jnp.where((t >= ka0) & (t < g0), za_blocks, 0)
    return jnp.where(t == n_main, _OFF_IF // REPACK_TN, moved)


def _repack_body(a_ref, b_ref, o_ref):
    t = pl.program_id(0)
    n_aligned = _OFF_IF // REPACK_TN
    n_main = MAIN_COLS // REPACK_TN
    n_slabs = REPACK_TN // LANES
    sh = _OFF_QA - _OFF_IF
    lane = lax.broadcasted_iota(jnp.int32, (a_ref.shape[0], LANES), 1)

    @pl.when(t < n_aligned)
    def _():
        o_ref[...] = a_ref[...].astype(o_ref.dtype)

    @pl.when((t >= n_aligned) & (t < n_main))
    def _():
        for c in range(n_slabs):
            lo = a_ref[:, c * LANES:(c + 1) * LANES]
            hi = a_ref[:, (c + 1) * LANES:(c + 2) * LANES] if c + 1 < n_slabs else b_ref[...]
            moved = jnp.where(lane < LANES - sh, pltpu.roll(lo, LANES - sh, 1), pltpu.roll(hi, LANES - sh, 1))
            o_ref[:, c * LANES:(c + 1) * LANES] = moved.astype(o_ref.dtype)

    @pl.when(t == n_main)
    def _():
        o_ref[...] = jnp.zeros_like(o_ref)
        o_ref[:, :LANES] = jnp.where(lane < sh, a_ref[:, :LANES], 0.0).astype(o_ref.dtype)


def _repack(w):
    n_tiles = MAIN_COLS // REPACK_TN + 1
    per = REPACK_TN // LANES
    return pl.pallas_call(
        _repack_body,
        grid=(n_tiles, D_MODEL // REPACK_ROWS),
        in_specs=[pl.BlockSpec((REPACK_ROWS, REPACK_TN), lambda t, r: (r, _repack_src_block(t))),
                  pl.BlockSpec((REPACK_ROWS, LANES), lambda t, r: (r, (_repack_src_block(t) + 1) * per))],
        out_specs=pl.BlockSpec((REPACK_ROWS, REPACK_TN), lambda t, r: (r, t)),
        out_shape=jax.ShapeDtypeStruct((D_MODEL, n_tiles * REPACK_TN), BF16),
        compiler_params=_params(("parallel", "parallel")),
        name="repack_w_in",
    )(w, w)


def _rope_slab(slab, cos, sin, lo):
    partner = jnp.where(lo, pltpu.roll(slab, LANES - 32, 1), pltpu.roll(slab, 32, 1))
    return slab * cos + partner * sin


def _proj_body(x_ref, w_ref, *rest, mode):
    acc = jnp.dot(x_ref[...], w_ref[...], preferred_element_type=F32)
    if mode == "plain":
        (o_ref,) = rest
        o_ref[...] = acc.astype(o_ref.dtype)
    elif mode == "sigmoid":
        (o_ref,) = rest
        o_ref[...] = jax.nn.sigmoid(acc).astype(o_ref.dtype)
    elif mode == "mlstm":
        (o_ref,) = rest
        j = pl.program_id(0)
        per = M_WIDTH // acc.shape[1]

        @pl.when(j < 3 * per)
        def _():
            o_ref[...] = acc.astype(o_ref.dtype)

        @pl.when((j >= 3 * per) & (j < 4 * per))
        def _():
            o_ref[...] = jax.nn.sigmoid(acc).astype(o_ref.dtype)

        @pl.when(j >= 4 * per)
        def _():
            o_ref[...] = jax.nn.silu(acc).astype(o_ref.dtype)
    else:
        cos_ref, sin_ref, o_ref = rest
        tm, tn = acc.shape
        j = pl.program_id(0)
        n_q = A_WIDTH // tn
        n_plain = n_q + A_WIDTH // tn
        rope_w = A_KV_WIDTH

        def store(width):
            cos = cos_ref[...]
            sin = sin_ref[...]
            lo = (lax.broadcasted_iota(jnp.int32, (tm, LANES), 1) % A_HEAD_DIM) < (A_HEAD_DIM // 2)
            for c in range(tn // LANES):
                slab = acc[:, c * LANES:(c + 1) * LANES]
                if c * LANES < width:
                    slab = _rope_slab(slab, cos, sin, lo)
                o_ref[:, c * LANES:(c + 1) * LANES] = slab.astype(o_ref.dtype)

        @pl.when(j < n_q)
        def _():
            store(tn)

        @pl.when((j >= n_q) & (j < n_plain))
        def _():
            o_ref[...] = jax.nn.silu(acc).astype(o_ref.dtype)

        @pl.when(j >= n_plain)
        def _():
            store(rope_w)


def _project(xb, w, col_block0, n_blocks, *, tn, out_dtype, mode, cos=None, sin=None, name):
    m_rows = xb.shape[0]
    tm = min(PROJ_TM, m_rows)
    in_specs = [pl.BlockSpec((tm, D_MODEL), lambda j, i: (i, 0)),
                pl.BlockSpec((D_MODEL, tn), lambda j, i: (0, j + col_block0))]
    args = [xb, w]
    if mode == "attn":
        in_specs += [pl.BlockSpec((tm, LANES), lambda j, i: (i, 0))] * 2
        args += [cos, sin]
    return pl.pallas_call(
        functools.partial(_proj_body, mode=mode),
        grid=(n_blocks, m_rows // tm),
        in_specs=in_specs,
        out_specs=pl.BlockSpec((tm, tn), lambda j, i: (i, j)),
        out_shape=jax.ShapeDtypeStruct((m_rows, n_blocks * tn), out_dtype),
        compiler_params=_params(("parallel", "parallel")),
        name=name,
    )(*args)


def _log_sigmoid(x):
    return jnp.minimum(x, 0.0) - jnp.log(1.0 + jnp.exp(-jnp.abs(x)))


def _cumsum_rows(a):
    n = a.shape[0]
    row = lax.broadcasted_iota(jnp.int32, a.shape, 0)
    shift = 1
    while shift < n:
        a = a + jnp.where(row >= shift, pltpu.roll(a, shift, 0), 0.0)
        shift *= 2
    return a


def _mlstm_body(*refs, chunk, n_seq, has_init):
    q_ref, k_ref, v_ref, o_ref, z_ref, g_ref, bif_ref, ng_ref = refs[:8]
    if has_init:
        c0_ref, n0_ref, m0_ref = refs[8:11]
        mix_ref, c_ref, n_ref, m_ref = refs[11:]
    else:
        mix_ref, c_ref, n_ref, m_ref = refs[8:]
        c0_ref, n0_ref, m0_ref = c_ref, n_ref, m_ref

        @pl.when(pl.program_id(1) == 0)
        def _():
            c_ref[...] = jnp.zeros_like(c_ref)
            n_ref[...] = jnp.zeros_like(n_ref)
            m_ref[...] = jnp.zeros_like(m_ref)

    L = chunk
    dh = M_HEAD_DIM
    lane = lax.broadcasted_iota(jnp.int32, (L, LANES), 1)
    causal = (lax.broadcasted_iota(jnp.int32, (L, L), 1) <= lax.broadcasted_iota(jnp.int32, (L, L), 0))
    pad_rows = (-L) % LANES

    gate_cols, gate_rows = [], []
    for s in range(n_seq):
        g = g_ref[s * L:(s + 1) * L, :] + bif_ref[...]
        b = _cumsum_rows(jnp.where(lane >= M_HEADS, _log_sigmoid(g), 0.0))
        col = jnp.where(lane < M_HEADS, g, b)
        colp = jnp.concatenate([col, jnp.zeros((pad_rows, LANES), F32)], axis=0) if pad_rows else col
        gate_cols.append(col)
        gate_rows.append(colp.T)

    def per_seq(ref, cols):
        a = ref[:, cols]
        if n_seq == 1:
            return [a]
        a = a.astype(F32)
        return [a[s * L:(s + 1) * L] for s in range(n_seq)]

    for h in range(M_HEADS):
        cols = slice(h * dh, (h + 1) * dh)
        qs, ks, vs = per_seq(q_ref, cols), per_seq(k_ref, cols), per_seq(v_ref, cols)
        os_, zs = per_seq(o_ref, cols), per_seq(z_ref, cols)
        mix_parts = []
        for s in range(n_seq):
            q = qs[s]
            k = ks[s] * (dh ** -0.5)
            qb, kb, vb = q.astype(BF16), k.astype(BF16), vs[s].astype(BF16)
            i_c = gate_cols[s][:, h:h + 1]
            b_c = gate_cols[s][:, M_HEADS + h:M_HEADS + h + 1]
            i_r = gate_rows[s][h:h + 1, :L]
            b_r = gate_rows[s][M_HEADS + h:M_HEADS + h + 1, :L]
            m_prev = m0_ref[s, h:h + 1, 0:1]
            c_prev = c0_ref[s, h]
            n_prev = n0_ref[s, h:h + 1, :]

            log_d = jnp.where(causal, (b_c - b_r) + i_r, -jnp.inf)
            a_c = b_c + m_prev
            m_t = jnp.maximum(a_c, jnp.max(log_d, axis=1, keepdims=True))
            sc = lax.dot_general(qb, kb, (((1,), (1,)), ((), ())), preferred_element_type=F32)
            sc = sc * jnp.exp(log_d - m_t)
            inter = jnp.exp(a_c - m_t)
            num = (jnp.dot(sc.astype(BF16), vb, preferred_element_type=F32)
                   + inter * jnp.dot(qb, c_prev.astype(BF16), preferred_element_type=F32))
            qn = jnp.sum(q.astype(F32) * n_prev, axis=1, keepdims=True)
            den = jnp.sum(sc, axis=1, keepdims=True) + inter * qn
            hid = num / jnp.maximum(jnp.abs(den), jnp.exp(-m_t))

            m_new = m_t[L - 1:L, :]
            b_last = b_c[L - 1:L, :]
            w_c = jnp.exp((b_last - b_c) + i_c - m_new)
            decay = jnp.exp(b_last + m_prev - m_new)
            kw = k.astype(F32) * w_c
            c_ref[s, h] = decay * c_prev + lax.dot_general(
                kw.astype(BF16), vb, (((0,), (0,)), ((), ())), preferred_element_type=F32)
            n_ref[s, h:h + 1, :] = decay * n_prev + jnp.sum(kw, axis=0, keepdims=True)
            m_ref[s, h:h + 1, :] = jnp.broadcast_to(m_new, (1, LANES))

            hid = os_[s].astype(F32) * hid
            mu = jnp.mean(hid, axis=1, keepdims=True)
            var = jnp.mean(jnp.square(hid - mu), axis=1, keepdims=True)
            hid = (hid - mu) * lax.rsqrt(var + LN_EPS) * ng_ref[:, cols]
            mix_parts.append(hid * zs[s].astype(F32))
        mix = mix_parts[0] if n_seq == 1 else jnp.concatenate(mix_parts, axis=0)
        mix_ref[:, cols] = mix.astype(mix_ref.dtype)


def _mlstm(pm, gates, bif, ng, *, n_batch, seq, chunk, n_seq, row0, init=None, name):
    rows = n_seq * chunk
    n_chunks = seq // chunk
    blk0 = row0 // rows

    def tok(col):
        return lambda b, c: (blk0 + b * n_chunks + c, col)

    in_specs = [pl.BlockSpec((rows, M_WIDTH), tok(g)) for g in range(5)]
    in_specs += [pl.BlockSpec((rows, LANES), tok(0)),
                 pl.BlockSpec((1, LANES), lambda b, c: (0, 0)),
                 pl.BlockSpec((1, M_WIDTH), lambda b, c: (0, 0))]
    args = [pm, pm, pm, pm, pm, gates, bif, ng]
    st_specs = [pl.BlockSpec((n_seq, M_HEADS, M_HEAD_DIM, M_HEAD_DIM), lambda b, c: (b, 0, 0, 0)),
                pl.BlockSpec((n_seq, M_HEADS, M_HEAD_DIM), lambda b, c: (b, 0, 0)),
                pl.BlockSpec((n_seq, M_HEADS, LANES), lambda b, c: (b, 0, 0))]
    if init is not None:
        in_specs += st_specs
        args += list(init)
    return pl.pallas_call(
        functools.partial(_mlstm_body, chunk=chunk, n_seq=n_seq, has_init=init is not None),
        grid=(n_batch // n_seq, n_chunks),
        in_specs=in_specs,
        out_specs=[pl.BlockSpec((rows, M_WIDTH), lambda b, c: (b * n_chunks + c, 0))] + st_specs,
        out_shape=[jax.ShapeDtypeStruct((n_batch * seq, M_WIDTH), BF16),
                   jax.ShapeDtypeStruct((n_batch, M_HEADS, M_HEAD_DIM, M_HEAD_DIM), F32),
                   jax.ShapeDtypeStruct((n_batch, M_HEADS, M_HEAD_DIM), F32),
                   jax.ShapeDtypeStruct((n_batch, M_HEADS, LANES), F32)],
        compiler_params=_params(("parallel", "arbitrary")),
        name=name,
    )(*args)


def _swa_prompt_body(sink_ref, q_ref, z_ref, kc_ref, kp_ref, vc_ref, vp_ref, o_ref):
    W = WINDOW
    hd = A_HEAD_DIM
    has_prev = pl.program_id(1) > 0
    row = lax.broadcasted_iota(jnp.int32, (W, LANES), 0)
    col = lax.broadcasted_iota(jnp.int32, (W, LANES), 1)
    causal = col <= row
    prev_ok = jnp.logical_and(jnp.logical_not(causal), has_prev)
    lo = col < hd
    nt = (((1,), (1,)), ((), ()))
    slabs_per_kv = A_GROUP * hd // LANES

    def swap_halves(a):
        return pltpu.roll(a.astype(F32), hd, 1).astype(BF16)

    for pair in range(A_KV_WIDTH // LANES):
        ks = slice(pair * LANES, (pair + 1) * LANES)
        kc, kp, vc, vp = kc_ref[:, ks], kp_ref[:, ks], vc_ref[:, ks], vp_ref[:, ks]
        kc_s, kp_s, vc_s, vp_s = swap_halves(kc), swap_halves(kp), swap_halves(vc), swap_halves(vp)
        zero = jnp.zeros_like(kc)
        for sub in range(2):
            at_lo = (kc, kp) if sub == 0 else (kc_s, kp_s)
            at_hi = (kc_s, kp_s) if sub == 0 else (kc, kp)
            keys = jnp.concatenate([jnp.where(lo, at_lo[0], zero), jnp.where(lo, at_lo[1], zero),
                                    jnp.where(lo, zero, at_hi[0]), jnp.where(lo, zero, at_hi[1])], axis=0)
            v_lo = jnp.concatenate([vc, vp] if sub == 0 else [vc_s, vp_s], axis=0)
            v_hi = jnp.concatenate([vc_s, vp_s] if sub == 0 else [vc, vp], axis=0)
            kvh = 2 * pair + sub
            for half in range(slabs_per_kv):
                slab = kvh * slabs_per_kv + half
                cs = slice(slab * LANES, (slab + 1) * LANES)
                qs = q_ref[:, cs] * (hd ** -0.5)
                sc_all = lax.dot_general(qs, keys, nt, preferred_element_type=F32)
                outs = []
                for par in range(2):
                    s_c = sc_all[:, (2 * par) * W:(2 * par + 1) * W]
                    s_p = sc_all[:, (2 * par + 1) * W:(2 * par + 2) * W]
                    sc = jnp.where(causal, s_c, jnp.where(prev_ok, s_p, -jnp.inf))
                    sink = sink_ref[2 * slab + par]
                    mx = jnp.maximum(jnp.max(sc, axis=1, keepdims=True), sink)
                    p = jnp.exp(sc - mx)
                    den = jnp.sum(p, axis=1, keepdims=True) + jnp.exp(sink - mx)
                    pb = jnp.concatenate([jnp.where(causal, p, 0.0), jnp.where(causal, 0.0, p)], axis=1)
                    pv = jnp.dot(pb.astype(BF16), v_lo if par == 0 else v_hi, preferred_element_type=F32)
                    outs.append(pv / den)
                att = jnp.where(lo, outs[0], outs[1])
                o_ref[:, cs] = (att * z_ref[:, cs].astype(F32)).astype(o_ref.dtype)


def _swa_prompt(pa, sinks, *, n_batch, seq, name):
    nb = seq // WINDOW
    kblk = A_WIDTH // A_KV_WIDTH

    def cur(col):
        return lambda b, n: (b * nb + n, col)

    def prev(col):
        return lambda b, n: (b * nb + jnp.maximum(n - 1, 0), col)

    return pl.pallas_call(
        _swa_prompt_body,
        grid=(n_batch, nb),
        in_specs=[pl.BlockSpec(memory_space=pltpu.SMEM),
                  pl.BlockSpec((WINDOW, A_WIDTH), cur(0)),
                  pl.BlockSpec((WINDOW, A_WIDTH), cur(1)),
                  pl.BlockSpec((WINDOW, A_KV_WIDTH), cur(2 * kblk)),
                  pl.BlockSpec((WINDOW, A_KV_WIDTH), prev(2 * kblk)),
                  pl.BlockSpec((WINDOW, A_KV_WIDTH), cur(2 * kblk + 1)),
                  pl.BlockSpec((WINDOW, A_KV_WIDTH), prev(2 * kblk + 1))],
        out_specs=pl.BlockSpec((WINDOW, A_WIDTH), cur(0)),
        out_shape=jax.ShapeDtypeStruct((n_batch * seq, A_WIDTH), BF16),
        compiler_params=_params(("parallel", "parallel")),
        name=name,
    )(sinks, pa, pa, pa, pa, pa, pa)


def _sink_softmax_pv(scores, values, sink):
    mx = jnp.maximum(sink, jnp.max(scores, axis=1, keepdims=True))
    p = jnp.exp(scores - mx)
    den = jnp.exp(sink - mx) + jnp.sum(p, axis=1, keepdims=True)
    return jnp.dot(p.astype(BF16), values, preferred_element_type=F32) / den


def _swa_sample_body(sink_ref, q_ref, z_ref, kn_ref, vn_ref, ck_ref, cv_ref, o_ref, ko_ref, vo_ref, *, n_seq, t_new):
    W = WINDOW
    hd = A_HEAD_DIM
    T = t_new
    R = A_GROUP * T
    S = 2 * W
    t_idx = lax.broadcasted_iota(jnp.int32, (R, S), 0) % T
    j_idx = lax.broadcasted_iota(jnp.int32, (R, S), 1)
    mask = ((j_idx < W) & (j_idx > t_idx)) | ((j_idx >= W) & (j_idx - W <= t_idx))
    nt = (((1,), (1,)), ((), ()))
    pad = jnp.zeros((S - W - T, A_KV_WIDTH), F32)
    q_all = q_ref[...].astype(F32)
    kn_all = kn_ref[...].astype(F32)
    vn_all = vn_ref[...].astype(F32)
    att_rows = []
    for s in range(n_seq):
        rows = slice(s * T, (s + 1) * T)
        kk = jnp.concatenate([ck_ref[s], kn_all[rows, :], pad], axis=0)
        vv = jnp.concatenate([cv_ref[s], vn_all[rows, :], pad], axis=0)
        ko_ref[s] = kk[T:T + W, :]
        vo_ref[s] = vv[T:T + W, :]
        kkb, vvb = kk.astype(BF16), vv.astype(BF16)
        outs = []
        for kh in range(A_KV_HEADS):
            heads = [kh * A_GROUP + g for g in range(A_GROUP)]
            q4 = jnp.concatenate([q_all[rows, h * hd:(h + 1) * hd] for h in heads], axis=0) * (hd ** -0.5)
            sink = jnp.concatenate([jnp.full((T, 1), sink_ref[h], F32) for h in heads], axis=0)
            kv = slice(kh * hd, (kh + 1) * hd)
            sc = lax.dot_general(q4.astype(BF16), kkb[:, kv], nt, preferred_element_type=F32)
            o4 = _sink_softmax_pv(jnp.where(mask, sc, -jnp.inf), vvb[:, kv], sink)
            outs += [o4[g * T:(g + 1) * T, :] for g in range(A_GROUP)]
        att_rows.append(jnp.concatenate(outs, axis=1))
    att = att_rows[0] if n_seq == 1 else jnp.concatenate(att_rows, axis=0)
    o_ref[...] = (att * z_ref[...].astype(F32)).astype(o_ref.dtype)


def _swa_sample(pa, cache_k, cache_v, sinks, *, n_batch, t_new, n_seq, row0, name):
    rows = n_seq * t_new
    blk0 = row0 // rows
    kblk = A_WIDTH // A_KV_WIDTH
    cache_spec = pl.BlockSpec((n_seq, WINDOW, A_KV_WIDTH), lambda b: (b, 0, 0))
    return pl.pallas_call(
        functools.partial(_swa_sample_body, n_seq=n_seq, t_new=t_new),
        grid=(n_batch // n_seq,),
        in_specs=[pl.BlockSpec(memory_space=pltpu.SMEM),
                  pl.BlockSpec((rows, A_WIDTH), lambda b: (blk0 + b, 0)),
                  pl.BlockSpec((rows, A_WIDTH), lambda b: (blk0 + b, 1)),
                  pl.BlockSpec((rows, A_KV_WIDTH), lambda b: (blk0 + b, 2 * kblk)),
                  pl.BlockSpec((rows, A_KV_WIDTH), lambda b: (blk0 + b, 2 * kblk + 1)),
                  cache_spec, cache_spec],
        out_specs=[pl.BlockSpec((rows, A_WIDTH), lambda b: (b, 0)), cache_spec, cache_spec],
        out_shape=[jax.ShapeDtypeStruct((n_batch * t_new, A_WIDTH), BF16),
                   jax.ShapeDtypeStruct(cache_k.shape, F32),
                   jax.ShapeDtypeStruct(cache_v.shape, F32)],
        compiler_params=_params(("parallel",)),
        name=name,
    )(sinks, pa, pa, pa, pa, cache_k, cache_v)


def _merge_body(mm_ref, ma_ref, wbm_ref, wba_ref, gm_ref, ga_ref, o_ref):
    bm = jnp.dot(mm_ref[...], wbm_ref[...], preferred_element_type=F32)
    ba = jnp.dot(ma_ref[...], wba_ref[...], preferred_element_type=F32)
    o_ref[...] = (gm_ref[...].astype(F32) * bm + ga_ref[...].astype(F32) * ba).astype(o_ref.dtype)


def _merge(mix_m, mix_a, w_bm, w_ba, gates, *, tn, row0, name):
    m_rows = mix_m.shape[0]
    tm = min(1024, m_rows)
    nj = D_MODEL // tn
    blk0 = row0 // tm
    return pl.pallas_call(
        _merge_body,
        grid=(m_rows // tm, nj),
        in_specs=[pl.BlockSpec((tm, M_WIDTH), lambda i, j: (i, 0)),
                  pl.BlockSpec((tm, A_WIDTH), lambda i, j: (i, 0)),
                  pl.BlockSpec((M_WIDTH, tn), lambda i, j: (0, j)),
                  pl.BlockSpec((A_WIDTH, tn), lambda i, j: (0, j)),
                  pl.BlockSpec((tm, tn), lambda i, j: (blk0 + i, j)),
                  pl.BlockSpec((tm, tn), lambda i, j: (blk0 + i, j + nj))],
        out_specs=pl.BlockSpec((tm, tn), lambda i, j: (i, j)),
        out_shape=jax.ShapeDtypeStruct((m_rows, D_MODEL), BF16),
        compiler_params=_params(("parallel", "parallel")),
        name=name,
    )(mix_m, mix_a, w_bm, w_ba, gates, gates)


def _out_body(mg_ref, w_ref, x_ref, g_ref, b_ref, o_ref, *, tn):
    j = pl.program_id(1)
    nj = o_ref.shape[1] // tn
    val = DEEPNORM_ALPHA * x_ref[...] + jnp.dot(mg_ref[...], w_ref[...], preferred_element_type=F32)
    for jj in range(nj):
        @pl.when(j == jj)
        def _(jj=jj):
            o_ref[:, jj * tn:(jj + 1) * tn] = val

    @pl.when(j == nj - 1)
    def _():
        def norm_rows(r, carry):
            rows = pl.ds(pl.multiple_of(r * SUBLANES, SUBLANES), SUBLANES)
            y = o_ref[rows, :]
            yc = y - jnp.mean(y, axis=1, keepdims=True)
            var = jnp.mean(jnp.square(yc), axis=1, keepdims=True)
            o_ref[rows, :] = yc * lax.rsqrt(var + LN_EPS) * g_ref[...] + b_ref[...]
            return carry

        lax.fori_loop(0, o_ref.shape[0] // SUBLANES, norm_rows, 0)


def _out_proj(merged, w_out, x2d, ln_g, ln_b, *, tm, tn, name):
    m_rows = merged.shape[0]
    return pl.pallas_call(
        functools.partial(_out_body, tn=tn),
        grid=(m_rows // tm, D_MODEL // tn),
        in_specs=[pl.BlockSpec((tm, D_MODEL), lambda i, j: (i, 0)),
                  pl.BlockSpec((D_MODEL, tn), lambda i, j: (0, j)),
                  pl.BlockSpec((tm, tn), lambda i, j: (i, j)),
                  pl.BlockSpec((1, D_MODEL), lambda i, j: (0, 0)),
                  pl.BlockSpec((1, D_MODEL), lambda i, j: (0, 0))],
        out_specs=pl.BlockSpec((tm, D_MODEL), lambda i, j: (i, 0)),
        out_shape=jax.ShapeDtypeStruct((m_rows, D_MODEL), F32),
        compiler_params=_params(("parallel", "arbitrary")),
        name=name,
    )(merged, w_out, x2d, ln_g, ln_b)


def _rope_tables(positions):
    half = A_HEAD_DIM // 2
    lane = jnp.arange(LANES)
    inv = ROPE_THETA ** (-(lane % half).astype(F32) / half)
    ang = positions.astype(F32)[:, None] * inv[None, :]
    sign = jnp.where((lane % A_HEAD_DIM) < half, -1.0, 1.0).astype(F32)
    return jnp.cos(ang), jnp.sin(ang) * sign[None, :]


def kernel(x_prompt, x_sample, state_C, state_n, state_m, cache_k, cache_v, w_in, b_if, norm_m_g,
           attn_sinks, w_bm, w_ba, w_out, ln_g, ln_b):
    bp, seq, _ = x_prompt.shape
    bs, t_new, _ = x_sample.shape
    w_buf = cache_k.shape[2]
    assert DEPTH == 1 and w_buf == WINDOW and seq % WINDOW == 0
    n_p, n_s = bp * seq, bs * t_new

    w_rep = _repack(w_in[0])
    bif = jnp.pad(b_if[0], (0, LANES - 2 * M_HEADS)).reshape(1, LANES)
    ng = norm_m_g[0].reshape(1, M_WIDTH)
    sinks = attn_sinks[0]
    wbm, wba, wo = w_bm[0].astype(BF16), w_ba[0].astype(BF16), w_out[0].astype(BF16)
    lng, lnb = ln_g[0].reshape(1, D_MODEL), ln_b[0].reshape(1, D_MODEL)

    xp2 = x_prompt.reshape(n_p, D_MODEL)
    xs2 = x_sample.reshape(n_s, D_MODEL)
    xb = jnp.concatenate([xp2, xs2], axis=0).astype(BF16)
    cos_p, sin_p = _rope_tables(jnp.arange(seq))
    cos_s, sin_s = _rope_tables(PAST_LEN + jnp.arange(t_new))
    cos = jnp.concatenate([jnp.tile(cos_p, (bp, 1)), jnp.tile(cos_s, (bs, 1))], axis=0)
    sin = jnp.concatenate([jnp.tile(sin_p, (bp, 1)), jnp.tile(sin_s, (bs, 1))], axis=0)
    tn = PROJ_TN
    b0 = 0
    pm = _project(xb, w_rep, b0, MLSTM_COLS // tn, tn=tn, out_dtype=BF16, mode="mlstm", name="proj_mlstm")
    b0 += MLSTM_COLS // tn
    pa = _project(xb, w_rep, b0, ATTN_COLS // tn, tn=tn, out_dtype=BF16, mode="attn", cos=cos, sin=sin,
                  name="proj_attn")
    b0 += ATTN_COLS // tn
    gates = _project(xb, w_rep, b0, GATE_COLS // tn, tn=tn, out_dtype=BF16, mode="sigmoid", name="proj_gate")
    g_if = _project(xb, w_rep, MAIN_COLS // LANES, 1, tn=LANES, out_dtype=F32, mode="plain", name="proj_if")

    mix_m, p_c, p_n, p_m = _mlstm(pm, g_if, bif, ng, n_batch=bp, seq=seq, chunk=128, n_seq=1, row0=0,
                                  name="mlstm_p")
    mix_a = _swa_prompt(pa, sinks, n_batch=bp, seq=seq, name="swa_p")
    merged = _merge(mix_m, mix_a, wbm, wba, gates, tn=512, row0=0, name="merge_p")
    y_p = _out_proj(merged, wo, xp2, lng, lnb, tm=512, tn=512, name="out_p").reshape(bp, seq, D_MODEL)
    kv_last = pa[:n_p].reshape(bp, seq, ATTN_COLS)[:, seq - w_buf:, 2 * A_WIDTH:].astype(F32)
    p_k = kv_last[..., :A_KV_WIDTH].reshape(1, bp, w_buf, A_KV_HEADS, A_HEAD_DIM)
    p_v = kv_last[..., A_KV_WIDTH:].reshape(1, bp, w_buf, A_KV_HEADS, A_HEAD_DIM)

    init = (state_C[0], state_n[0], jnp.broadcast_to(state_m[0][:, :, None], (bs, M_HEADS, LANES)))
    mix_m, s_c, s_n, s_m = _mlstm(pm, g_if, bif, ng, n_batch=bs, seq=t_new, chunk=t_new, n_seq=2, row0=n_p,
                                  init=init, name="mlstm_s")
    mix_a, s_k, s_v = _swa_sample(pa, cache_k[0].reshape(bs, w_buf, A_KV_WIDTH),
                                  cache_v[0].reshape(bs, w_buf, A_KV_WIDTH), sinks,
                                  n_batch=bs, t_new=t_new, n_seq=2, row0=n_p, name="swa_s")
    merged = _merge(mix_m, mix_a, wbm, wba, gates, tn=512, row0=n_p, name="merge_s")
    y_s = _out_proj(merged, wo, xs2, lng, lnb, tm=512, tn=512, name="out_s").reshape(bs, t_new, D_MODEL)

    def st(a, like):
        return a[None].astype(like.dtype)

    return (y_p, y_s,
            st(p_c, state_C), st(p_n, state_n), st(p_m[:, :, 0], state_m), p_k, p_v,
            st(s_c, state_C), st(s_n, state_n), st(s_m[:, :, 0], state_m),
            s_k.reshape(1, bs, w_buf, A_KV_HEADS, A_HEAD_DIM), s_v.reshape(1, bs, w_buf, A_KV_HEADS, A_HEAD_DIM))
```

```python
import functools

import jax
import jax.numpy as jnp
from jax import lax
from jax.experimental import pallas as pl
from jax.experimental.pallas import tpu as pltpu

F32 = jnp.float32
BF16 = jnp.bfloat16

D_MODEL = 4096
DEPTH = 1
PAST_LEN = 8192
M_HEADS = 8
M_WIDTH = D_MODEL // 2
M_HEAD_DIM = M_WIDTH // M_HEADS
A_HEAD_DIM = 64
A_WIDTH = D_MODEL // 2
A_HEADS = A_WIDTH // A_HEAD_DIM
A_KV_HEADS = A_HEADS // 4
A_GROUP = A_HEADS // A_KV_HEADS
A_KV_WIDTH = A_KV_HEADS * A_HEAD_DIM
WINDOW = 128
ROPE_THETA = 10000.0
LN_EPS = 1e-5
DEEPNORM_ALPHA = (2.0 * DEPTH) ** 0.25

LANES = 128
SUBLANES = 8
VMEM_LIMIT = 56 * 1024 * 1024

_OFF_IF = 5 * M_WIDTH
_OFF_QA = _OFF_IF + 2 * M_HEADS

PROJ_TM = 1024
PROJ_TN = 1024
REPACK_TN = 512
REPACK_ROWS = 2048
MLSTM_COLS = 5 * M_WIDTH
ATTN_COLS = 2 * A_WIDTH + 2 * A_KV_WIDTH
GATE_COLS = 2 * D_MODEL
MAIN_COLS = MLSTM_COLS + ATTN_COLS + GATE_COLS


def _params(sem):
    return pltpu.CompilerParams(dimension_semantics=sem, vmem_limit_bytes=VMEM_LIMIT)


def _repack_src_block(t):
    n_main = MAIN_COLS // REPACK_TN
    za0 = (MLSTM_COLS + A_WIDTH) // REPACK_TN
    ka0 = za0 + A_WIDTH // REPACK_TN
    g0 = ka0 + 2 * A_KV_WIDTH // REPACK_TN
    kv_blocks = 2 * A_KV_WIDTH // REPACK_TN
    za_blocks = A_WIDTH // REPACK_TN
    moved = t + jnp.where((t >= za0) & (t < ka0), kv_blocks, 0) - jnp.where((t >= ka0) & (t < g0), za_blocks, 0)
    return jnp.where(t == n_main, _OFF_IF // REPACK_TN, moved)


def _repack_body(a_ref, b_ref, o_ref):
    t = pl.program_id(0)
    n_aligned = _OFF_IF // REPACK_TN
    n_main = MAIN_COLS // REPACK_TN
    sh = _OFF_QA - _OFF_IF

    @pl.when(t < n_aligned)
    def _():
        o_ref[...] = a_ref[...].T.astype(o_ref.dtype)

    @pl.when((t >= n_aligned) & (t < n_main))
    def _():
        moved = jnp.concatenate([a_ref[sh:, :], b_ref[...]], axis=0)
        o_ref[...] = moved.T.astype(o_ref.dtype)

    @pl.when(t == n_main)
    def _():
        row = lax.broadcasted_iota(jnp.int32, a_ref.shape, 0)
        o_ref[...] = jnp.where(row < sh, a_ref[...], 0.0).T.astype(o_ref.dtype)


def _repack(wt):
    n_tiles = MAIN_COLS // REPACK_TN + 1
    sh = _OFF_QA - _OFF_IF
    per = REPACK_TN // sh
    return pl.pallas_call(
        _repack_body,
        grid=(n_tiles, D_MODEL // REPACK_ROWS),
        in_specs=[pl.BlockSpec((REPACK_TN, REPACK_ROWS), lambda t, r: (_repack_src_block(t), r)),
                  pl.BlockSpec((sh, REPACK_ROWS), lambda t, r: ((_repack_src_block(t) + 1) * per, r))],
        out_specs=pl.BlockSpec((REPACK_ROWS, REPACK_TN), lambda t, r: (r, t)),
        out_shape=jax.ShapeDtypeStruct((D_MODEL, n_tiles * REPACK_TN), BF16),
        compiler_params=_params(("parallel", "parallel")),
        name="repack_w_in",
    )(wt, wt)


def _cast_tokens_body(xp_ref, xs_ref, o_ref, *, n_prompt_blocks):
    i = pl.program_id(0)

    @pl.when(i < n_prompt_blocks)
    def _():
        o_ref[...] = xp_ref[...].astype(o_ref.dtype)

    @pl.when(i >= n_prompt_blocks)
    def _():
        o_ref[...] = xs_ref[...].astype(o_ref.dtype)


def _cast_tokens(xp2, xs2, *, tm):
    n_p, n_s = xp2.shape[0] // tm, xs2.shape[0] // tm
    return pl.pallas_call(
        functools.partial(_cast_tokens_body, n_prompt_blocks=n_p),
        grid=(n_p + n_s,),
        in_specs=[pl.BlockSpec((tm, D_MODEL), lambda i: (jnp.minimum(i, n_p - 1), 0)),
                  pl.BlockSpec((tm, D_MODEL), lambda i: (jnp.maximum(i - n_p, 0), 0))],
        out_specs=pl.BlockSpec((tm, D_MODEL), lambda i: (i, 0)),
        out_shape=jax.ShapeDtypeStruct((xp2.shape[0] + xs2.shape[0], D_MODEL), BF16),
        compiler_params=_params(("parallel",)),
        name="cast_tokens",
    )(xp2, xs2)


def _rope_slab(slab, cos, sin, lo):
    partner = jnp.where(lo, pltpu.roll(slab, LANES - 32, 1), pltpu.roll(slab, 32, 1))
    return slab * cos + partner * sin


def _proj_body(x_ref, w_ref, *rest, mode):
    acc = jnp.dot(x_ref[...], w_ref[...], preferred_element_type=F32)
    if mode == "plain":
        (o_ref,) = rest
        o_ref[...] = acc.astype(o_ref.dtype)
    elif mode == "sigmoid":
        (o_ref,) = rest
        o_ref[...] = jax.nn.sigmoid(acc).astype(o_ref.dtype)
    elif mode == "mlstm":
        (o_ref,) = rest
        j = pl.program_id(0)
        per = M_WIDTH // acc.shape[1]

        @pl.when(j < 3 * per)
        def _():
            o_ref[...] = acc.astype(o_ref.dtype)

        @pl.when((j >= 3 * per) & (j < 4 * per))
        def _():
            o_ref[...] = jax.nn.sigmoid(acc).astype(o_ref.dtype)

        @pl.when(j >= 4 * per)
        def _():
            o_ref[...] = jax.nn.silu(acc).astype(o_ref.dtype)
    else:
        cos_ref, sin_ref, o_ref = rest
        tm, tn = acc.shape
        j = pl.program_id(0)
        n_q = A_WIDTH // tn
        n_plain = n_q + A_WIDTH // tn
        rope_w = A_KV_WIDTH

        def store(width):
            cos = cos_ref[...]
            sin = sin_ref[...]
            lo = (lax.broadcasted_iota(jnp.int32, (tm, LANES), 1) % A_HEAD_DIM) < (A_HEAD_DIM // 2)
            for c in range(tn // LANES):
                slab = acc[:, c * LANES:(c + 1) * LANES]
                if c * LANES < width:
                    slab = _rope_slab(slab, cos, sin, lo)
                o_ref[:, c * LANES:(c + 1) * LANES] = slab.astype(o_ref.dtype)

        @pl.when(j < n_q)
        def _():
            store(tn)

        @pl.when((j >= n_q) & (j < n_plain))
        def _():
            o_ref[...] = jax.nn.silu(acc).astype(o_ref.dtype)

        @pl.when(j >= n_plain)
        def _():
            store(rope_w)


def _project(xb, w, col_block0, n_blocks, *, tn, out_dtype, mode, cos=None, sin=None, name):
    m_rows = xb.shape[0]
    tm = min(PROJ_TM, m_rows)
    in_specs = [pl.BlockSpec((tm, D_MODEL), lambda j, i: (i, 0)),
                pl.BlockSpec((D_MODEL, tn), lambda j, i: (0, j + col_block0))]
    args = [xb, w]
    if mode == "attn":
        in_specs += [pl.BlockSpec((tm, LANES), lambda j, i: (i, 0))] * 2
        args += [cos, sin]
    return pl.pallas_call(
        functools.partial(_proj_body, mode=mode),
        grid=(n_blocks, m_rows // tm),
        in_specs=in_specs,
        out_specs=pl.BlockSpec((tm, tn), lambda j, i: (i, j)),
        out_shape=jax.ShapeDtypeStruct((m_rows, n_blocks * tn), out_dtype),
        compiler_params=_params(("parallel", "parallel")),
        name=name,
    )(*args)


def _log_sigmoid(x):
    return jnp.minimum(x, 0.0) - jnp.log(1.0 + jnp.exp(-jnp.abs(x)))


def _cumsum_rows(a):
    n = a.shape[0]
    row = lax.broadcasted_iota(jnp.int32, a.shape, 0)
    shift = 1
    while shift < n:
        a = a + jnp.where(row >= shift, pltpu.roll(a, shift, 0), 0.0)
        shift *= 2
    return a


def _mlstm_body(*refs, chunk, n_seq, has_init):
    q_ref, k_ref, v_ref, o_ref, z_ref, g_ref, bif_ref, ng_ref = refs[:8]
    if has_init:
        c0_ref, n0_ref, m0_ref = refs[8:11]
        mix_ref, c_ref, n_ref, m_ref = refs[11:]
    else:
        mix_ref, c_ref, n_ref, m_ref = refs[8:]
        c0_ref, n0_ref, m0_ref = c_ref, n_ref, m_ref

        @pl.when(pl.program_id(1) == 0)
        def _():
            c_ref[...] = jnp.zeros_like(c_ref)
            n_ref[...] = jnp.zeros_like(n_ref)
            m_ref[...] = jnp.zeros_like(m_ref)

    L = chunk
    dh = M_HEAD_DIM
    lane = lax.broadcasted_iota(jnp.int32, (L, LANES), 1)
    causal = (lax.broadcasted_iota(jnp.int32, (L, L), 1) <= lax.broadcasted_iota(jnp.int32, (L, L), 0))
    pad_rows = (-L) % LANES

    gate_cols, gate_rows = [], []
    for s in range(n_seq):
        g = g_ref[s * L:(s + 1) * L, :] + bif_ref[...]
        b = _cumsum_rows(jnp.where(lane >= M_HEADS, _log_sigmoid(g), 0.0))
        col = jnp.where(lane < M_HEADS, g, b)
        colp = jnp.concatenate([col, jnp.zeros((pad_rows, LANES), F32)], axis=0) if pad_rows else col
        gate_cols.append(col)
        gate_rows.append(colp.T)

    def per_seq(ref, cols):
        a = ref[:, cols]
        if n_seq == 1:
            return [a]
        a = a.astype(F32)
        return [a[s * L:(s + 1) * L] for s in range(n_seq)]

    for h in range(M_HEADS):
        cols = slice(h * dh, (h + 1) * dh)
        qs, ks, vs = per_seq(q_ref, cols), per_seq(k_ref, cols), per_seq(v_ref, cols)
        os_, zs = per_seq(o_ref, cols), per_seq(z_ref, cols)
        mix_parts = []
        for s in range(n_seq):
            q = qs[s]
            k = ks[s] * (dh ** -0.5)
            qb, kb, vb = q.astype(BF16), k.astype(BF16), vs[s].astype(BF16)
            i_c = gate_cols[s][:, h:h + 1]
            b_c = gate_cols[s][:, M_HEADS + h:M_HEADS + h + 1]
            i_r = gate_rows[s][h:h + 1, :L]
            b_r = gate_rows[s][M_HEADS + h:M_HEADS + h + 1, :L]
            m_prev = m0_ref[s, h:h + 1, 0:1]
            c_prev = c0_ref[s, h]
            n_prev = n0_ref[s, h:h + 1, :]

            log_d = jnp.where(causal, (b_c - b_r) + i_r, -jnp.inf)
            a_c = b_c + m_prev
            m_t = jnp.maximum(a_c, jnp.max(log_d, axis=1, keepdims=True))
            sc = lax.dot_general(qb, kb, (((1,), (1,)), ((), ())), preferred_element_type=F32)
            sc = sc * jnp.exp(log_d - m_t)
            inter = jnp.exp(a_c - m_t)
            num = (jnp.dot(sc.astype(BF16), vb, preferred_element_type=F32)
                   + inter * jnp.dot(qb, c_prev.astype(BF16), preferred_element_type=F32))
            qn = jnp.sum(q.astype(F32) * n_prev, axis=1, keepdims=True)
            den = jnp.sum(sc, axis=1, keepdims=True) + inter * qn
            hid = num / jnp.maximum(jnp.abs(den), jnp.exp(-m_t))

            m_new = m_t[L - 1:L, :]
            b_last = b_c[L - 1:L, :]
            w_c = jnp.exp((b_last - b_c) + i_c - m_new)
            decay = jnp.exp(b_last + m_prev - m_new)
            kw = k.astype(F32) * w_c
            c_ref[s, h] = decay * c_prev + lax.dot_general(
                kw.astype(BF16), vb, (((0,), (0,)), ((), ())), preferred_element_type=F32)
            n_ref[s, h:h + 1, :] = decay * n_prev + jnp.sum(kw, axis=0, keepdims=True)
            m_ref[s, h:h + 1, :] = jnp.broadcast_to(m_new, (1, LANES))

            hid = os_[s].astype(F32) * hid
            mu = jnp.mean(hid, axis=1, keepdims=True)
            var = jnp.mean(jnp.square(hid - mu), axis=1, keepdims=True)
            hid = (hid - mu) * lax.rsqrt(var + LN_EPS) * ng_ref[:, cols]
            mix_parts.append(hid * zs[s].astype(F32))
        mix = mix_parts[0] if n_seq == 1 else jnp.concatenate(mix_parts, axis=0)
        mix_ref[:, cols] = mix.astype(mix_ref.dtype)


def _mlstm(pm, gates, bif, ng, *, n_batch, seq, chunk, n_seq, row0, init=None, name):
    rows = n_seq * chunk
    n_chunks = seq // chunk
    blk0 = row0 // rows

    def tok(col):
        return lambda b, c: (blk0 + b * n_chunks + c, col)

    in_specs = [pl.BlockSpec((rows, M_WIDTH), tok(g)) for g in range(5)]
    in_specs += [pl.BlockSpec((rows, LANES), tok(0)),
                 pl.BlockSpec((1, LANES), lambda b, c: (0, 0)),
                 pl.BlockSpec((1, M_WIDTH), lambda b, c: (0, 0))]
    args = [pm, pm, pm, pm, pm, gates, bif, ng]
    st_specs = [pl.BlockSpec((n_seq, M_HEADS, M_HEAD_DIM, M_HEAD_DIM), lambda b, c: (b, 0, 0, 0)),
                pl.BlockSpec((n_seq, M_HEADS, M_HEAD_DIM), lambda b, c: (b, 0, 0)),
                pl.BlockSpec((n_seq, M_HEADS, LANES), lambda b, c: (b, 0, 0))]
    if init is not None:
        in_specs += st_specs
        args += list(init)
    return pl.pallas_call(
        functools.partial(_mlstm_body, chunk=chunk, n_seq=n_seq, has_init=init is not None),
        grid=(n_batch // n_seq, n_chunks),
        in_specs=in_specs,
        out_specs=[pl.BlockSpec((rows, M_WIDTH), lambda b, c: (b * n_chunks + c, 0))] + st_specs,
        out_shape=[jax.ShapeDtypeStruct((n_batch * seq, M_WIDTH), BF16),
                   jax.ShapeDtypeStruct((n_batch, M_HEADS, M_HEAD_DIM, M_HEAD_DIM), F32),
                   jax.ShapeDtypeStruct((n_batch, M_HEADS, M_HEAD_DIM), F32),
                   jax.ShapeDtypeStruct((n_batch, M_HEADS, LANES), F32)],
        compiler_params=_params(("parallel", "arbitrary")),
        name=name,
    )(*args)


def _swa_prompt_body(sink_ref, q_ref, z_ref, kc_ref, kp_ref, vc_ref, vp_ref, o_ref):
    W = WINDOW
    hd = A_HEAD_DIM
    has_prev = pl.program_id(1) > 0
    row = lax.broadcasted_iota(jnp.int32, (W, LANES), 0)
    col = lax.broadcasted_iota(jnp.int32, (W, LANES), 1)
    causal = col <= row
    prev_ok = jnp.logical_and(jnp.logical_not(causal), has_prev)
    lo = col < hd
    nt = (((1,), (1,)), ((), ()))
    slabs_per_kv = A_GROUP * hd // LANES

    def swap_halves(a):
        return pltpu.roll(a.astype(F32), hd, 1).astype(BF16)

    for pair in range(A_KV_WIDTH // LANES):
        ks = slice(pair * LANES, (pair + 1) * LANES)
        kc, kp, vc, vp = kc_ref[:, ks], kp_ref[:, ks], vc_ref[:, ks], vp_ref[:, ks]
        kc_s, kp_s, vc_s, vp_s = swap_halves(kc), swap_halves(kp), swap_halves(vc), swap_halves(vp)
        zero = jnp.zeros_like(kc)
        for sub in range(2):
            at_lo = (kc, kp) if sub == 0 else (kc_s, kp_s)
            at_hi = (kc_s, kp_s) if sub == 0 else (kc, kp)
            keys = jnp.concatenate([jnp.where(lo, at_lo[0], zero), jnp.where(lo, at_lo[1], zero),
                                    jnp.where(lo, zero, at_hi[0]), jnp.where(lo, zero, at_hi[1])], axis=0)
            v_lo = jnp.concatenate([vc, vp] if sub == 0 else [vc_s, vp_s], axis=0)
            v_hi = jnp.concatenate([vc_s, vp_s] if sub == 0 else [vc, vp], axis=0)
            kvh = 2 * pair + sub
            for half in range(slabs_per_kv):
                slab = kvh * slabs_per_kv + half
                cs = slice(slab * LANES, (slab + 1) * LANES)
                qs = q_ref[:, cs] * (hd ** -0.5)
                sc_all = lax.dot_general(qs, keys, nt, preferred_element_type=F32)
                outs = []
                for par in range(2):
                    s_c = sc_all[:, (2 * par) * W:(2 * par + 1) * W]
                    s_p = sc_all[:, (2 * par + 1) * W:(2 * par + 2) * W]
                    sc = jnp.where(causal, s_c, jnp.where(prev_ok, s_p, -jnp.inf))
                    sink = sink_ref[2 * slab + par]
                    mx = jnp.maximum(jnp.max(sc, axis=1, keepdims=True), sink)
                    p = jnp.exp(sc - mx)
                    den = jnp.sum(p, axis=1, keepdims=True) + jnp.exp(sink - mx)
                    pb = jnp.concatenate([jnp.where(causal, p, 0.0), jnp.where(causal, 0.0, p)], axis=1)
                    pv = jnp.dot(pb.astype(BF16), v_lo if par == 0 else v_hi, preferred_element_type=F32)
                    outs.append(pv / den)
                att = jnp.where(lo, outs[0], outs[1])
                o_ref[:, cs] = (att * z_ref[:, cs].astype(F32)).astype(o_ref.dtype)


def _swa_prompt(pa, sinks, *, n_batch, seq, name):
    nb = seq // WINDOW
    kblk = A_WIDTH // A_KV_WIDTH

    def cur(col):
        return lambda b, n: (b * nb + n, col)

    def prev(col):
        return lambda b, n: (b * nb + jnp.maximum(n - 1, 0), col)

    return pl.pallas_call(
        _swa_prompt_body,
        grid=(n_batch, nb),
        in_specs=[pl.BlockSpec(memory_space=pltpu.SMEM),
                  pl.BlockSpec((WINDOW, A_WIDTH), cur(0)),
                  pl.BlockSpec((WINDOW, A_WIDTH), cur(1)),
                  pl.BlockSpec((WINDOW, A_KV_WIDTH), cur(2 * kblk)),
                  pl.BlockSpec((WINDOW, A_KV_WIDTH), prev(2 * kblk)),
                  pl.BlockSpec((WINDOW, A_KV_WIDTH), cur(2 * kblk + 1)),
                  pl.BlockSpec((WINDOW, A_KV_WIDTH), prev(2 * kblk + 1))],
        out_specs=pl.BlockSpec((WINDOW, A_WIDTH), cur(0)),
        out_shape=jax.ShapeDtypeStruct((n_batch * seq, A_WIDTH), BF16),
        compiler_params=_params(("parallel", "parallel")),
        name=name,
    )(sinks, pa, pa, pa, pa, pa, pa)


def _sink_softmax_pv(scores, values, sink):
    mx = jnp.maximum(sink, jnp.max(scores, axis=1, keepdims=True))
    p = jnp.exp(scores - mx)
    den = jnp.exp(sink - mx) + jnp.sum(p, axis=1, keepdims=True)
    return jnp.dot(p.astype(BF16), values, preferred_element_type=F32) / den


def _swa_sample_body(sink_ref, q_ref, z_ref, kn_ref, vn_ref, ck_ref, cv_ref, o_ref, ko_ref, vo_ref, *, n_seq, t_new):
    W = WINDOW
    hd = A_HEAD_DIM
    T = t_new
    R = A_GROUP * T
    S = 2 * W
    t_idx = lax.broadcasted_iota(jnp.int32, (R, S), 0) % T
    j_idx = lax.broadcasted_iota(jnp.int32, (R, S), 1)
    mask = ((j_idx < W) & (j_idx > t_idx)) | ((j_idx >= W) & (j_idx - W <= t_idx))
    nt = (((1,), (1,)), ((), ()))
    pad = jnp.zeros((S - W - T, A_KV_WIDTH), F32)
    q_all = q_ref[...].astype(F32)
    kn_all = kn_ref[...].astype(F32)
    vn_all = vn_ref[...].astype(F32)
    att_rows = []
    for s in range(n_seq):
        rows = slice(s * T, (s + 1) * T)
        kk = jnp.concatenate([ck_ref[s], kn_all[rows, :], pad], axis=0)
        vv = jnp.concatenate([cv_ref[s], vn_all[rows, :], pad], axis=0)
        ko_ref[s] = kk[T:T + W, :]
        vo_ref[s] = vv[T:T + W, :]
        kkb, vvb = kk.astype(BF16), vv.astype(BF16)
        outs = []
        for kh in range(A_KV_HEADS):
            heads = [kh * A_GROUP + g for g in range(A_GROUP)]
            q4 = jnp.concatenate([q_all[rows, h * hd:(h + 1) * hd] for h in heads], axis=0) * (hd ** -0.5)
            sink = jnp.concatenate([jnp.full((T, 1), sink_ref[h], F32) for h in heads], axis=0)
            kv = slice(kh * hd, (kh + 1) * hd)
            sc = lax.dot_general(q4.astype(BF16), kkb[:, kv], nt, preferred_element_type=F32)
            o4 = _sink_softmax_pv(jnp.where(mask, sc, -jnp.inf), vvb[:, kv], sink)
            outs += [o4[g * T:(g + 1) * T, :] for g in range(A_GROUP)]
        att_rows.append(jnp.concatenate(outs, axis=1))
    att = att_rows[0] if n_seq == 1 else jnp.concatenate(att_rows, axis=0)
    o_ref[...] = (att * z_ref[...].astype(F32)).astype(o_ref.dtype)


def _swa_sample(pa, cache_k, cache_v, sinks, *, n_batch, t_new, n_seq, row0, name):
    rows = n_seq * t_new
    blk0 = row0 // rows
    kblk = A_WIDTH // A_KV_WIDTH
    cache_spec = pl.BlockSpec((n_seq, WINDOW, A_KV_WIDTH), lambda b: (b, 0, 0))
    return pl.pallas_call(
        functools.partial(_swa_sample_body, n_seq=n_seq, t_new=t_new),
        grid=(n_batch // n_seq,),
        in_specs=[pl.BlockSpec(memory_space=pltpu.SMEM),
                  pl.BlockSpec((rows, A_WIDTH), lambda b: (blk0 + b, 0)),
                  pl.BlockSpec((rows, A_WIDTH), lambda b: (blk0 + b, 1)),
                  pl.BlockSpec((rows, A_KV_WIDTH), lambda b: (blk0 + b, 2 * kblk)),
                  pl.BlockSpec((rows, A_KV_WIDTH), lambda b: (blk0 + b, 2 * kblk + 1)),
                  cache_spec, cache_spec],
        out_specs=[pl.BlockSpec((rows, A_WIDTH), lambda b: (b, 0)), cache_spec, cache_spec],
        out_shape=[jax.ShapeDtypeStruct((n_batch * t_new, A_WIDTH), BF16),
                   jax.ShapeDtypeStruct(cache_k.shape, F32),
                   jax.ShapeDtypeStruct(cache_v.shape, F32)],
        compiler_params=_params(("parallel",)),
        name=name,
    )(sinks, pa, pa, pa, pa, cache_k, cache_v)


def _merge_body(mm_ref, ma_ref, wbm_ref, wba_ref, gm_ref, ga_ref, o_ref):
    bm = jnp.dot(mm_ref[...], wbm_ref[...], preferred_element_type=F32)
    ba = jnp.dot(ma_ref[...], wba_ref[...], preferred_element_type=F32)
    o_ref[...] = (gm_ref[...].astype(F32) * bm + ga_ref[...].astype(F32) * ba).astype(o_ref.dtype)


def _merge(mix_m, mix_a, w_bm, w_ba, gates, *, tn, row0, name):
    m_rows = mix_m.shape[0]
    tm = min(1024, m_rows)
    nj = D_MODEL // tn
    blk0 = row0 // tm
    return pl.pallas_call(
        _merge_body,
        grid=(m_rows // tm, nj),
        in_specs=[pl.BlockSpec((tm, M_WIDTH), lambda i, j: (i, 0)),
                  pl.BlockSpec((tm, A_WIDTH), lambda i, j: (i, 0)),
                  pl.BlockSpec((M_WIDTH, tn), lambda i, j: (0, j)),
                  pl.BlockSpec((A_WIDTH, tn), lambda i, j: (0, j)),
                  pl.BlockSpec((tm, tn), lambda i, j: (blk0 + i, j)),
                  pl.BlockSpec((tm, tn), lambda i, j: (blk0 + i, j + nj))],
        out_specs=pl.BlockSpec((tm, tn), lambda i, j: (i, j)),
        out_shape=jax.ShapeDtypeStruct((m_rows, D_MODEL), BF16),
        compiler_params=_params(("parallel", "parallel")),
        name=name,
    )(mix_m, mix_a, w_bm, w_ba, gates, gates)


def _out_body(mg_ref, w_ref, x_ref, g_ref, b_ref, o_ref, *, tn):
    j = pl.program_id(1)
    nj = o_ref.shape[1] // tn
    val = DEEPNORM_ALPHA * x_ref[...] + jnp.dot(mg_ref[...], w_ref[...], preferred_element_type=F32)
    for jj in range(nj):
        @pl.when(j == jj)
        def _(jj=jj):
            o_ref[:, jj * tn:(jj + 1) * tn] = val

    @pl.when(j == nj - 1)
    def _():
        chunk = 8 * SUBLANES

        def norm_rows(r, carry):
            rows = pl.ds(pl.multiple_of(r * chunk, chunk), chunk)
            y = o_ref[rows, :]
            yc = y - jnp.mean(y, axis=1, keepdims=True)
            var = jnp.mean(jnp.square(yc), axis=1, keepdims=True)
            o_ref[rows, :] = yc * lax.rsqrt(var + LN_EPS) * g_ref[...] + b_ref[...]
            return carry

        lax.fori_loop(0, o_ref.shape[0] // chunk, norm_rows, 0)


def _out_proj(merged, w_out, x2d, ln_g, ln_b, *, tm, tn, name):
    m_rows = merged.shape[0]
    return pl.pallas_call(
        functools.partial(_out_body, tn=tn),
        grid=(m_rows // tm, D_MODEL // tn),
        in_specs=[pl.BlockSpec((tm, D_MODEL), lambda i, j: (i, 0)),
                  pl.BlockSpec((D_MODEL, tn), lambda i, j: (0, j)),
                  pl.BlockSpec((tm, tn), lambda i, j: (i, j)),
                  pl.BlockSpec((1, D_MODEL), lambda i, j: (0, 0)),
                  pl.BlockSpec((1, D_MODEL), lambda i, j: (0, 0))],
        out_specs=pl.BlockSpec((tm, D_MODEL), lambda i, j: (i, 0)),
        out_shape=jax.ShapeDtypeStruct((m_rows, D_MODEL), F32),
        compiler_params=_params(("parallel", "arbitrary")),
        name=name,
    )(merged, w_out, x2d, ln_g, ln_b)


def _rope_tables(positions):
    half = A_HEAD_DIM // 2
    lane = jnp.arange(LANES)
    inv = ROPE_THETA ** (-(lane % half).astype(F32) / half)
    ang = positions.astype(F32)[:, None] * inv[None, :]
    sign = jnp.where((lane % A_HEAD_DIM) < half, -1.0, 1.0).astype(F32)
    return jnp.cos(ang), jnp.sin(ang) * sign[None, :]


def kernel(x_prompt, x_sample, state_C, state_n, state_m, cache_k, cache_v, w_in, b_if, norm_m_g,
           attn_sinks, w_bm, w_ba, w_out, ln_g, ln_b):
    bp, seq, _ = x_prompt.shape
    bs, t_new, _ = x_sample.shape
    w_buf = cache_k.shape[2]
    assert DEPTH == 1 and w_buf == WINDOW and seq % WINDOW == 0
    n_p, n_s = bp * seq, bs * t_new

    w_rep = _repack(jnp.swapaxes(w_in[0], 0, 1))
    bif = jnp.pad(b_if[0], (0, LANES - 2 * M_HEADS)).reshape(1, LANES)
    ng = norm_m_g[0].reshape(1, M_WIDTH)
    sinks = attn_sinks[0]
    wbm, wba, wo = w_bm[0].astype(BF16), w_ba[0].astype(BF16), w_out[0].astype(BF16)
    lng, lnb = ln_g[0].reshape(1, D_MODEL), ln_b[0].reshape(1, D_MODEL)

    xp2 = x_prompt.reshape(n_p, D_MODEL)
    xs2 = x_sample.reshape(n_s, D_MODEL)
    xb = _cast_tokens(xp2, xs2, tm=512)
    cos_p, sin_p = _rope_tables(jnp.arange(seq))
    cos_s, sin_s = _rope_tables(PAST_LEN + jnp.arange(t_new))
    cos = jnp.concatenate([jnp.tile(cos_p, (bp, 1)), jnp.tile(cos_s, (bs, 1))], axis=0)
    sin = jnp.concatenate([jnp.tile(sin_p, (bp, 1)), jnp.tile(sin_s, (bs, 1))], axis=0)
    tn = PROJ_TN
    b0 = 0
    pm = _project(xb, w_rep, b0, MLSTM_COLS // tn, tn=tn, out_dtype=BF16, mode="mlstm", name="proj_mlstm")
    b0 += MLSTM_COLS // tn
    pa = _project(xb, w_rep, b0, ATTN_COLS // tn, tn=tn, out_dtype=BF16, mode="attn", cos=cos, sin=sin,
                  name="proj_attn")
    b0 += ATTN_COLS // tn
    gates = _project(xb, w_rep, b0, GATE_COLS // tn, tn=tn, out_dtype=BF16, mode="sigmoid", name="proj_gate")
    g_if = _project(xb, w_rep, MAIN_COLS // LANES, 1, tn=LANES, out_dtype=F32, mode="plain", name="proj_if")

    mix_m, p_c, p_n, p_m = _mlstm(pm, g_if, bif, ng, n_batch=bp, seq=seq, chunk=128, n_seq=1, row0=0,
                                  name="mlstm_p")
    mix_a = _swa_prompt(pa, sinks, n_batch=bp, seq=seq, name="swa_p")
    merged = _merge(mix_m, mix_a, wbm, wba, gates, tn=512, row0=0, name="merge_p")
    y_p = _out_proj(merged, wo, xp2, lng, lnb, tm=512, tn=512, name="out_p").reshape(bp, seq, D_MODEL)
    kv_last = pa[:n_p].reshape(bp, seq, ATTN_COLS)[:, seq - w_buf:, 2 * A_WIDTH:].astype(F32)
    p_k = kv_last[..., :A_KV_WIDTH].reshape(1, bp, w_buf, A_KV_HEADS, A_HEAD_DIM)
    p_v = kv_last[..., A_KV_WIDTH:].reshape(1, bp, w_buf, A_KV_HEADS, A_HEAD_DIM)

    init = (state_C[0], state_n[0], jnp.broadcast_to(state_m[0][:, :, None], (bs, M_HEADS, LANES)))
    mix_m, s_c, s_n, s_m = _mlstm(pm, g_if, bif, ng, n_batch=bs, seq=t_new, chunk=t_new, n_seq=2, row0=n_p,
                                  init=init, name="mlstm_s")
    mix_a, s_k, s_v = _swa_sample(pa, cache_k[0].reshape(bs, w_buf, A_KV_WIDTH),
                                  cache_v[0].reshape(bs, w_buf, A_KV_WIDTH), sinks,
                                  n_batch=bs, t_new=t_new, n_seq=2, row0=n_p, name="swa_s")
    merged = _merge(mix_m, mix_a, wbm, wba, gates, tn=512, row0=n_p, name="merge_s")
    y_s = _out_proj(merged, wo, xs2, lng, lnb, tm=512, tn=512, name="out_s").reshape(bs, t_new, D_MODEL)

    def st(a, like):
        return a[None].astype(like.dtype)

    return (y_p, y_s,
            st(p_c, state_C), st(p_n, state_n), st(p_m[:, :, 0], state_m), p_k, p_v,
            st(s_c, state_C), st(s_n, state_n), st(s_m[:, :, 0], state_m),
            s_k.reshape(1, bs, w_buf, A_KV_HEADS, A_HEAD_DIM), s_v.reshape(1, bs, w_buf, A_KV_HEADS, A_HEAD_DIM))
```

```python
import functools

import jax
import jax.numpy as jnp
from jax import lax
from jax.experimental import pallas as pl
from jax.experimental.pallas import tpu as pltpu

F32 = jnp.float32
BF16 = jnp.bfloat16

D_MODEL = 4096
DEPTH = 1
PAST_LEN = 8192
M_HEADS = 8
M_WIDTH = D_MODEL // 2
M_HEAD_DIM = M_WIDTH // M_HEADS
A_HEAD_DIM = 64
A_WIDTH = D_MODEL // 2
A_HEADS = A_WIDTH // A_HEAD_DIM
A_KV_HEADS = A_HEADS // 4
A_GROUP = A_HEADS // A_KV_HEADS
A_KV_WIDTH = A_KV_HEADS * A_HEAD_DIM
WINDOW = 128
ROPE_THETA = 10000.0
LN_EPS = 1e-5
DEEPNORM_ALPHA = (2.0 * DEPTH) ** 0.25

LANES = 128
SUBLANES = 8
VMEM_LIMIT = 60 * 1024 * 1024

_OFF_IF = 5 * M_WIDTH
_OFF_QA = _OFF_IF + 2 * M_HEADS

PROJ_TM = 1024
PROJ_TN = 1024
MLSTM_TN = 1280
REPACK_TN = 512
REPACK_ROWS = 2048
MLSTM_COLS = 5 * M_WIDTH
ATTN_COLS = 2 * A_WIDTH + 2 * A_KV_WIDTH
GATE_COLS = 2 * D_MODEL
MAIN_COLS = MLSTM_COLS + ATTN_COLS + GATE_COLS


def _params(sem):
    return pltpu.CompilerParams(dimension_semantics=sem, vmem_limit_bytes=VMEM_LIMIT)


def _repack_src_block(t):
    n_main = MAIN_COLS // REPACK_TN
    za0 = (MLSTM_COLS + A_WIDTH) // REPACK_TN
    ka0 = za0 + A_WIDTH // REPACK_TN
    g0 = ka0 + 2 * A_KV_WIDTH // REPACK_TN
    kv_blocks = 2 * A_KV_WIDTH // REPACK_TN
    za_blocks = A_WIDTH // REPACK_TN
    moved = t + jnp.where((t >= za0) & (t < ka0), kv_blocks, 0) - jnp.where((t >= ka0) & (t < g0), za_blocks, 0)
    return jnp.where(t == n_main, _OFF_IF // REPACK_TN, moved)


def _repack_block(t, a_ref, b_ref, o_ref):
    n_aligned = _OFF_IF // REPACK_TN
    n_main = MAIN_COLS // REPACK_TN
    sh = _OFF_QA - _OFF_IF
    a = a_ref[...]
    moved = jnp.concatenate([a[sh:, :], b_ref[...]], axis=0)
    row = lax.broadcasted_iota(jnp.int32, a.shape, 0)
    gate_rows = jnp.where(row < sh, a, 0.0)
    src = jnp.where(t < n_aligned, a, jnp.where(t < n_main, moved, gate_rows))
    o_ref[...] = src.T.astype(o_ref.dtype)


_ATTN_TILE0 = MLSTM_COLS // REPACK_TN
_ATTN_TILES = ATTN_COLS // REPACK_TN
_FIRST_TILES = _ATTN_TILES + 1
_REST_TILES = MAIN_COLS // REPACK_TN - _ATTN_TILES
_ROW_HALVES = D_MODEL // REPACK_ROWS
_KV_TILES = 2 * A_KV_WIDTH // REPACK_TN


def _repack_specs(tile_of, half_of, out_tile_of):
    per = REPACK_TN // (_OFF_QA - _OFF_IF)
    return dict(
        in_specs=[pl.BlockSpec((REPACK_TN, REPACK_ROWS), lambda *g: (_repack_src_block(tile_of(*g)), half_of(*g))),
                  pl.BlockSpec((_OFF_QA - _OFF_IF, REPACK_ROWS),
                               lambda *g: ((_repack_src_block(tile_of(*g)) + 1) * per, half_of(*g)))],
        out_specs=[pl.BlockSpec((REPACK_ROWS, REPACK_TN), lambda *g: (half_of(*g), out_tile_of(*g)))])


def _repack_attn(wt):
    def tile_of(u, r=None):
        kv0 = _ATTN_TILE0 + _ATTN_TILES - _KV_TILES
        return jnp.where(u < _KV_TILES, kv0 + u,
                         jnp.where(u < _ATTN_TILES, _ATTN_TILE0 + u - _KV_TILES, MAIN_COLS // REPACK_TN))

    specs = _repack_specs(tile_of, lambda u, r: r, lambda u, r: u)

    def body(a_ref, b_ref, o_ref):
        _repack_block(tile_of(pl.program_id(0)), a_ref, b_ref, o_ref)

    return pl.pallas_call(
        body,
        grid=(_FIRST_TILES, _ROW_HALVES),
        in_specs=specs["in_specs"],
        out_specs=specs["out_specs"][0],
        out_shape=jax.ShapeDtypeStruct((D_MODEL, _FIRST_TILES * REPACK_TN), BF16),
        compiler_params=_params(("parallel", "parallel")),
        name="repack_attn",
    )(wt, wt)


def _repack_rest_side(wt, *, n_steps):
    n_blocks = _REST_TILES * _ROW_HALVES
    assert n_steps >= n_blocks

    def local(s):
        return jnp.minimum(s, n_blocks - 1) // _ROW_HALVES

    def tile_of(s):
        return local(s) + jnp.where(local(s) >= _ATTN_TILE0, _ATTN_TILES, 0)

    def half_of(s):
        return jnp.minimum(s, n_blocks - 1) % _ROW_HALVES

    def fn(step, in_refs, out_refs):
        _repack_block(tile_of(step), *in_refs, *out_refs)
        yield

    return dict(fn=fn, args=[wt, wt], out_shape=[jax.ShapeDtypeStruct((D_MODEL, _REST_TILES * REPACK_TN), BF16)],
                **_repack_specs(tile_of, half_of, local))


def _hosted_body(*refs, n_main_in, n_side_in, n_main_out, main_fn, side_fn):
    main_in = refs[:n_main_in]
    side_in = refs[n_main_in:n_main_in + n_side_in]
    main_out = refs[n_main_in + n_side_in:n_main_in + n_side_in + n_main_out]
    side_out = refs[n_main_in + n_side_in + n_main_out:]
    step = pl.program_id(0)
    stages = [fn(step, i, o) for fn, i, o in ((side_fn, side_in, side_out), (main_fn, main_in, main_out)) if fn]
    while stages:
        alive = []
        for g in stages:
            try:
                next(g)
                alive.append(g)
            except StopIteration:
                pass
        stages = alive


def _join_sides(*sides):
    def fn(step, in_refs, out_refs):
        gens, i0, o0 = [], 0, 0
        for sd in sides:
            ni, no = len(sd["args"]), len(sd["out_shape"])
            gens.append(sd["fn"](step, in_refs[i0:i0 + ni], out_refs[o0:o0 + no]))
            i0, o0 = i0 + ni, o0 + no
        while gens:
            alive = []
            for g in gens:
                try:
                    next(g)
                    alive.append(g)
                except StopIteration:
                    pass
            gens = alive
            if gens:
                yield

    return dict(fn=fn, **{k: sum((sd[k] for sd in sides), []) for k in ("args", "in_specs", "out_specs", "out_shape")})


def _cast_rows_side(mats, *, n_slabs):
    def fn(step, in_refs, out_refs):
        for src, dst in zip(in_refs, out_refs):
            dst[...] = src[...].astype(dst.dtype)
        yield

    specs = [pl.BlockSpec((m.shape[0] // n_slabs, m.shape[1]), lambda s: (jnp.minimum(s, n_slabs - 1), 0))
             for m in mats]
    return dict(fn=fn, args=list(mats), in_specs=specs, out_specs=list(specs),
                out_shape=[jax.ShapeDtypeStruct(m.shape, BF16) for m in mats])


def _gate_proj_side(xb, w, col_block, *, n_steps):
    rows = xb.shape[0] // n_steps
    assert rows * n_steps == xb.shape[0] and rows % SUBLANES == 0

    def fn(step, in_refs, out_refs):
        out_refs[0][...] = jnp.dot(in_refs[0][...], in_refs[1][...], preferred_element_type=F32)
        yield

    return dict(fn=fn, args=[xb, w],
                in_specs=[pl.BlockSpec((rows, D_MODEL), lambda s: (s, 0)),
                          pl.BlockSpec((D_MODEL, LANES), lambda s: (0, col_block))],
                out_specs=[pl.BlockSpec((rows, LANES), lambda s: (s, 0))],
                out_shape=[jax.ShapeDtypeStruct((xb.shape[0], LANES), F32)])


def _hosted_call(n_steps, main, side, name):
    side = side or dict(fn=None, args=[], in_specs=[], out_specs=[], out_shape=[])
    outs = pl.pallas_call(
        functools.partial(_hosted_body, n_main_in=len(main["args"]), n_side_in=len(side["args"]),
                          n_main_out=len(main["out_shape"]), main_fn=main["fn"], side_fn=side["fn"]),
        grid=(n_steps,),
        in_specs=main["in_specs"] + side["in_specs"],
        out_specs=main["out_specs"] + side["out_specs"],
        out_shape=main["out_shape"] + side["out_shape"],
        compiler_params=_params(("arbitrary",)),
        name=name,
    )(*main["args"], *side["args"])
    n_main = len(main["out_shape"])
    return outs[:n_main], outs[n_main:]


def _rope_slab(slab, cos, sin, lo):
    partner = jnp.where(lo, pltpu.roll(slab, LANES - 32, 1), pltpu.roll(slab, 32, 1))
    return slab * cos + partner * sin


def _proj_tile(j, x_ref, w_ref, rest, mode):
    tm, tn = x_ref.shape[0], w_ref.shape[1]
    pw = next((c for c in (4 * LANES, 2 * LANES) if tn % c == 0), tn)
    n_parts = tn // pw
    o_ref = rest[-1]

    if mode == "plain":
        def epilogue(acc, col0):
            return acc
    elif mode == "sigmoid":
        def epilogue(acc, col0):
            return jax.nn.sigmoid(acc)
    elif mode == "mlstm":
        per = M_WIDTH // LANES

        def epilogue(acc, col0):
            sig = jax.nn.sigmoid(acc)
            outs = []
            for c in range(acc.shape[1] // LANES):
                slab = j * (tn // LANES) + col0 // LANES + c
                a, s = acc[:, c * LANES:(c + 1) * LANES], sig[:, c * LANES:(c + 1) * LANES]
                outs.append(jnp.where(slab < 3 * per, a, jnp.where(slab < 4 * per, s, a * s)))
            return jnp.concatenate(outs, axis=1)
    else:
        cos_ref, sin_ref, _ = rest
        n_q = A_WIDTH // tn
        n_plain = n_q + A_WIDTH // tn
        rope_w = A_KV_WIDTH
        lo = (lax.broadcasted_iota(jnp.int32, (tm, LANES), 1) % A_HEAD_DIM) < (A_HEAD_DIM // 2)

        def epilogue(acc, col0):
            outs = []
            for c in range(acc.shape[1] // LANES):
                slab = acc[:, c * LANES:(c + 1) * LANES]
                roped = _rope_slab(slab, cos_ref[...], sin_ref[...], lo)
                use_rope = (j < n_q) | (j >= n_plain) if col0 + c * LANES < rope_w else j < n_q
                outs.append(jnp.where(use_rope, roped, jnp.where(j < n_plain, slab * jax.nn.sigmoid(slab), slab)))
            return jnp.concatenate(outs, axis=1)

    for p in range(n_parts):
        acc = jnp.dot(x_ref[...], w_ref[:, p * pw:(p + 1) * pw], preferred_element_type=F32)
        o_ref[:, p * pw:(p + 1) * pw] = epilogue(acc, p * pw).astype(o_ref.dtype)
        yield


def _project(xb, w, col_block0, n_blocks, *, tn, out_dtype, mode, tm=PROJ_TM, cos=None, sin=None, side=None,
             name):
    m_rows = xb.shape[0]
    tm = min(tm, m_rows)
    n_m = m_rows // tm
    in_specs = [pl.BlockSpec((tm, D_MODEL), lambda s: (s % n_m, 0)),
                pl.BlockSpec((D_MODEL, tn), lambda s: (0, s // n_m + col_block0))]
    args = [xb, w]
    if mode == "attn":
        in_specs += [pl.BlockSpec((tm, LANES), lambda s: (s % n_m, 0))] * 2
        args += [cos, sin]

    def fn(step, in_refs, out_refs):
        return _proj_tile(step // n_m, in_refs[0], in_refs[1], tuple(in_refs[2:]) + tuple(out_refs), mode)

    main = dict(fn=fn, args=args, in_specs=in_specs,
                out_specs=[pl.BlockSpec((tm, tn), lambda s: (s % n_m, s // n_m))],
                out_shape=[jax.ShapeDtypeStruct((m_rows, n_blocks * tn), out_dtype)])
    (out,), side_outs = _hosted_call(n_blocks * n_m, main, side, name)
    return out, side_outs


def _project_first(xp2, xs2, w, cos, sin, *, tm, tn, name):
    n_p, n_s = xp2.shape[0] // tm, xs2.shape[0] // tm
    kv_tile = 2 * (A_WIDTH // tn)

    def body(xp_ref, xs_ref, w_ref, cos_ref, sin_ref, xb_ref, o_ref):
        i = pl.program_id(0)
        xb_ref[...] = jnp.where(i < n_p, xp_ref[...], xs_ref[...]).astype(xb_ref.dtype)
        for _ in _proj_tile(kv_tile, xb_ref, w_ref, (cos_ref, sin_ref, o_ref), "attn"):
            pass

    return pl.pallas_call(
        body,
        grid=(n_p + n_s,),
        in_specs=[pl.BlockSpec((tm, D_MODEL), lambda i: (jnp.minimum(i, n_p - 1), 0)),
                  pl.BlockSpec((tm, D_MODEL), lambda i: (jnp.maximum(i - n_p, 0), 0)),
                  pl.BlockSpec((D_MODEL, tn), lambda i: (0, 0)),
                  pl.BlockSpec((tm, LANES), lambda i: (i, 0)),
                  pl.BlockSpec((tm, LANES), lambda i: (i, 0))],
        out_specs=[pl.BlockSpec((tm, D_MODEL), lambda i: (i, 0)),
                   pl.BlockSpec((tm, tn), lambda i: (i, 0))],
        out_shape=[jax.ShapeDtypeStruct(((n_p + n_s) * tm, D_MODEL), BF16),
                   jax.ShapeDtypeStruct(((n_p + n_s) * tm, tn), BF16)],
        compiler_params=_params(("arbitrary",)),
        name=name,
    )(xp2, xs2, w, cos, sin)


def _log_sigmoid(x):
    return jnp.minimum(x, 0.0) - jnp.log(1.0 + jnp.exp(-jnp.abs(x)))


def _cumsum_rows(a):
    n = a.shape[0]
    row = lax.broadcasted_iota(jnp.int32, a.shape, 0)
    shift = 1
    while shift < n:
        a = a + jnp.where(row >= shift, pltpu.roll(a, shift, 0), 0.0)
        shift *= 2
    return a


def _mlstm_chunk(first_chunk, refs, *, chunk, n_seq, has_init):
    q_ref, k_ref, v_ref, o_ref, z_ref, g_ref, bif_ref, ng_ref = refs[:8]
    if has_init:
        c0_ref, n0_ref, m0_ref = refs[8:11]
        mix_ref, c_ref, n_ref, m_ref = refs[11:]
    else:
        mix_ref, c_ref, n_ref, m_ref = refs[8:]
        c0_ref, n0_ref, m0_ref = c_ref, n_ref, m_ref

        @pl.when(first_chunk)
        def _():
            c_ref[...] = jnp.zeros_like(c_ref)
            n_ref[...] = jnp.zeros_like(n_ref)
            m_ref[...] = jnp.zeros_like(m_ref)

    L = chunk
    dh = M_HEAD_DIM
    lane = lax.broadcasted_iota(jnp.int32, (L, LANES), 1)
    causal = (lax.broadcasted_iota(jnp.int32, (L, L), 1) <= lax.broadcasted_iota(jnp.int32, (L, L), 0))
    pad_rows = (-L) % LANES

    gate_cols, gate_rows = [], []
    for s in range(n_seq):
        g = g_ref[s * L:(s + 1) * L, :] + bif_ref[...]
        b = _cumsum_rows(jnp.where(lane >= M_HEADS, _log_sigmoid(g), 0.0))
        col = jnp.where(lane < M_HEADS, g, b)
        colp = jnp.concatenate([col, jnp.zeros((pad_rows, LANES), F32)], axis=0) if pad_rows else col
        gate_cols.append(col)
        gate_rows.append(colp.T)

    def per_seq(ref, cols):
        a = ref[:, cols]
        if n_seq == 1:
            return [a]
        a = a.astype(F32)
        return [a[s * L:(s + 1) * L] for s in range(n_seq)]

    jobs = []
    for h in range(M_HEADS):
        cols = slice(h * dh, (h + 1) * dh)
        qs, ks, vs = per_seq(q_ref, cols), per_seq(k_ref, cols), per_seq(v_ref, cols)
        for s in range(n_seq):
            q = qs[s]
            k = ks[s] * (dh ** -0.5)
            qb, kb, vb = q.astype(BF16), k.astype(BF16), vs[s].astype(BF16)
            i_c = gate_cols[s][:, h:h + 1]
            b_c = gate_cols[s][:, M_HEADS + h:M_HEADS + h + 1]
            i_r = gate_rows[s][h:h + 1, :L]
            b_r = gate_rows[s][M_HEADS + h:M_HEADS + h + 1, :L]
            m_prev = m0_ref[s, h:h + 1, 0:1]

            log_d = jnp.where(causal, (b_c - b_r) + i_r, -jnp.inf)
            a_c = b_c + m_prev
            m_t = jnp.maximum(a_c, jnp.max(log_d, axis=1, keepdims=True))
            m_new = m_t[L - 1:L, :]
            b_last = b_c[L - 1:L, :]
            w_c = jnp.exp((b_last - b_c) + i_c - m_new)
            kw = k.astype(F32) * w_c
            jobs.append(dict(
                h=h, s=s, q=q, vb=vb, log_d=log_d, a_c=a_c, m_t=m_t, m_new=m_new, kw=kw,
                decay=jnp.exp(b_last + m_prev - m_new),
                qk=lax.dot_general(qb, kb, (((1,), (1,)), ((), ())), preferred_element_type=F32),
                qc=jnp.dot(qb, c0_ref[s, h].astype(BF16), preferred_element_type=F32)))
    yield

    for job in jobs:
        job["sc"] = job["qk"] * jnp.exp(job["log_d"] - job["m_t"])
        job["sv"] = jnp.dot(job["sc"].astype(BF16), job["vb"], preferred_element_type=F32)
    yield

    for h in range(M_HEADS):
        cols = slice(h * dh, (h + 1) * dh)
        os_, zs = per_seq(o_ref, cols), per_seq(z_ref, cols)
        mix_parts = []
        for s in range(n_seq):
            job = jobs[h * n_seq + s]
            n_prev = n0_ref[s, h:h + 1, :]
            inter = jnp.exp(job["a_c"] - job["m_t"])
            num = job["sv"] + inter * job["qc"]
            qn = jnp.sum(job["q"].astype(F32) * n_prev, axis=1, keepdims=True)
            den = jnp.sum(job["sc"], axis=1, keepdims=True) + inter * qn
            hid = num / jnp.maximum(jnp.abs(den), jnp.exp(-job["m_t"]))

            c_ref[s, h] = job["decay"] * c0_ref[s, h] + lax.dot_general(
                job["kw"].astype(BF16), job["vb"], (((0,), (0,)), ((), ())), preferred_element_type=F32)
            n_ref[s, h:h + 1, :] = job["decay"] * n_prev + jnp.sum(job["kw"], axis=0, keepdims=True)
            m_ref[s, h:h + 1, :] = jnp.broadcast_to(job["m_new"], (1, LANES))

            hid = os_[s].astype(F32) * hid
            mu = jnp.mean(hid, axis=1, keepdims=True)
            var = jnp.mean(jnp.square(hid - mu), axis=1, keepdims=True)
            hid = (hid - mu) * lax.rsqrt(var + LN_EPS) * ng_ref[:, cols]
            mix_parts.append(hid * zs[s].astype(F32))
        mix = mix_parts[0] if n_seq == 1 else jnp.concatenate(mix_parts, axis=0)
        mix_ref[:, cols] = mix.astype(mix_ref.dtype)


def _mlstm_state_shapes(n_slots):
    return [jax.ShapeDtypeStruct((n_slots, M_HEADS, M_HEAD_DIM, M_HEAD_DIM), F32),
            jax.ShapeDtypeStruct((n_slots, M_HEADS, M_HEAD_DIM), F32),
            jax.ShapeDtypeStruct((n_slots, M_HEADS, LANES), F32)]


def _mlstm_prompt_side(pm, gates, bif, ng, *, n_steps, seq, chunk):
    n_chunks = seq // chunk
    n_slots = -(-n_steps // n_chunks)
    assert n_steps * chunk <= pm.shape[0]
    st_specs = [pl.BlockSpec((1, M_HEADS, M_HEAD_DIM, M_HEAD_DIM), lambda s: (s // n_chunks, 0, 0, 0)),
                pl.BlockSpec((1, M_HEADS, M_HEAD_DIM), lambda s: (s // n_chunks, 0, 0)),
                pl.BlockSpec((1, M_HEADS, LANES), lambda s: (s // n_chunks, 0, 0))]

    def fn(step, in_refs, out_refs):
        return _mlstm_chunk(step % n_chunks == 0, tuple(in_refs) + tuple(out_refs), chunk=chunk, n_seq=1,
                            has_init=False)

    def grp(g):
        return pl.BlockSpec((chunk, M_WIDTH), lambda s: (s, g))

    return dict(fn=fn, args=[pm, pm, pm, pm, pm, gates, bif, ng],
                in_specs=[grp(g) for g in range(5)] + [pl.BlockSpec((chunk, LANES), lambda s: (s, 0)),
                                                       pl.BlockSpec((1, LANES), lambda s: (0, 0)),
                                                       pl.BlockSpec((1, M_WIDTH), lambda s: (0, 0))],
                out_specs=[pl.BlockSpec((chunk, M_WIDTH), lambda s: (s, 0))] + st_specs,
                out_shape=[jax.ShapeDtypeStruct((n_steps * chunk, M_WIDTH), BF16)] + _mlstm_state_shapes(n_slots))


def _mlstm_sample_side(pm, gates, bif, ng, init, *, n_steps, n_batch, t_new, row0):
    n_seq = n_batch // n_steps
    rows = n_seq * t_new
    blk0 = row0 // rows
    in_specs = [pl.BlockSpec((rows, M_WIDTH), lambda s, g=g: (blk0 + s, g)) for g in range(5)]
    in_specs += [pl.BlockSpec((rows, LANES), lambda s: (blk0 + s, 0)),
                 pl.BlockSpec((1, LANES), lambda s: (0, 0)),
                 pl.BlockSpec((1, M_WIDTH), lambda s: (0, 0))]
    st_specs = [pl.BlockSpec((n_seq, M_HEADS, M_HEAD_DIM, M_HEAD_DIM), lambda s: (s, 0, 0, 0)),
                pl.BlockSpec((n_seq, M_HEADS, M_HEAD_DIM), lambda s: (s, 0, 0)),
                pl.BlockSpec((n_seq, M_HEADS, LANES), lambda s: (s, 0, 0))]

    def fn(step, in_refs, out_refs):
        return _mlstm_chunk(None, tuple(in_refs) + tuple(out_refs), chunk=t_new, n_seq=n_seq, has_init=True)

    return dict(fn=fn, args=[pm, pm, pm, pm, pm, gates, bif, ng, *init],
                in_specs=in_specs + st_specs,
                out_specs=[pl.BlockSpec((rows, M_WIDTH), lambda s: (s, 0))] + st_specs,
                out_shape=[jax.ShapeDtypeStruct((n_batch * t_new, M_WIDTH), BF16)] + _mlstm_state_shapes(n_batch))


def _swa_prompt_block(has_prev, sink_ref, q_ref, z_ref, kc_ref, kp_ref, vc_ref, vp_ref, o_ref):
    W = WINDOW
    hd = A_HEAD_DIM
    row = lax.broadcasted_iota(jnp.int32, (W, LANES), 0)
    col = lax.broadcasted_iota(jnp.int32, (W, LANES), 1)
    causal = col <= row
    prev_ok = jnp.logical_and(jnp.logical_not(causal), has_prev)
    lo = col < hd
    nt = (((1,), (1,)), ((), ()))
    slabs_per_kv = A_GROUP * hd // LANES

    def swap_halves(a):
        return pltpu.roll(a.astype(F32), hd, 1).astype(BF16)

    jobs = []
    for pair in range(A_KV_WIDTH // LANES):
        ks = slice(pair * LANES, (pair + 1) * LANES)
        kc, kp, vc, vp = kc_ref[:, ks], kp_ref[:, ks], vc_ref[:, ks], vp_ref[:, ks]
        kc_s, kp_s, vc_s, vp_s = swap_halves(kc), swap_halves(kp), swap_halves(vc), swap_halves(vp)
        zero = jnp.zeros_like(kc)
        for sub in range(2):
            at_lo = (kc, kp) if sub == 0 else (kc_s, kp_s)
            at_hi = (kc_s, kp_s) if sub == 0 else (kc, kp)
            keys = jnp.concatenate([jnp.where(lo, at_lo[0], zero), jnp.where(lo, at_lo[1], zero),
                                    jnp.where(lo, zero, at_hi[0]), jnp.where(lo, zero, at_hi[1])], axis=0)
            v_lo = jnp.concatenate([vc, vp] if sub == 0 else [vc_s, vp_s], axis=0)
            v_hi = jnp.concatenate([vc_s, vp_s] if sub == 0 else [vc, vp], axis=0)
            kvh = 2 * pair + sub
            slabs = [kvh * slabs_per_kv + half for half in range(slabs_per_kv)]
            qs = jnp.concatenate([q_ref[:, sl * LANES:(sl + 1) * LANES] for sl in slabs], axis=0) * (hd ** -0.5)
            jobs.append(dict(slabs=slabs, values=(v_lo, v_hi),
                             sc_all=lax.dot_general(qs, keys, nt, preferred_element_type=F32)))
    yield

    causal_n = jnp.concatenate([causal] * slabs_per_kv, axis=0)
    prev_ok_n = jnp.concatenate([prev_ok] * slabs_per_kv, axis=0)
    for job in jobs:
        outs = []
        for par in range(2):
            s_c = job["sc_all"][:, (2 * par) * W:(2 * par + 1) * W]
            s_p = job["sc_all"][:, (2 * par + 1) * W:(2 * par + 2) * W]
            sc = jnp.where(causal_n, s_c, jnp.where(prev_ok_n, s_p, -jnp.inf))
            sink = jnp.concatenate([jnp.full((W, 1), sink_ref[2 * sl + par], F32) for sl in job["slabs"]], axis=0)
            mx = jnp.maximum(jnp.max(sc, axis=1, keepdims=True), sink)
            p = jnp.exp(sc - mx)
            den = jnp.sum(p, axis=1, keepdims=True) + jnp.exp(sink - mx)
            pb = jnp.concatenate([jnp.where(causal_n, p, 0.0), jnp.where(causal_n, 0.0, p)], axis=1)
            outs.append((jnp.dot(pb.astype(BF16), job["values"][par], preferred_element_type=F32), den))
        job["outs"] = outs
    yield

    for job in jobs:
        (pv0, den0), (pv1, den1) = job["outs"]
        for n, sl in enumerate(job["slabs"]):
            rows, cs = slice(n * W, (n + 1) * W), slice(sl * LANES, (sl + 1) * LANES)
            att = jnp.where(lo, pv0[rows] / den0[rows], pv1[rows] / den1[rows])
            o_ref[:, cs] = (att * z_ref[:, cs].astype(F32)).astype(o_ref.dtype)


def _swa_prompt_side(pa_qz, pa_kv, sinks, *, n_steps, n_batch, seq):
    nb = seq // WINDOW

    def cur(col):
        return lambda s: (jnp.minimum(s // nb, n_batch - 1) * nb + s % nb, col)

    def prev(col):
        return lambda s: (jnp.minimum(s // nb, n_batch - 1) * nb + jnp.maximum(s % nb - 1, 0), col)

    def fn(step, in_refs, out_refs):
        return _swa_prompt_block(step % nb > 0, *in_refs, *out_refs)

    return dict(fn=fn, args=[sinks, pa_qz, pa_qz, pa_kv, pa_kv, pa_kv, pa_kv],
                in_specs=[pl.BlockSpec(memory_space=pltpu.SMEM),
                          pl.BlockSpec((WINDOW, A_WIDTH), cur(0)),
                          pl.BlockSpec((WINDOW, A_WIDTH), cur(1)),
                          pl.BlockSpec((WINDOW, A_KV_WIDTH), cur(0)),
                          pl.BlockSpec((WINDOW, A_KV_WIDTH), prev(0)),
                          pl.BlockSpec((WINDOW, A_KV_WIDTH), cur(1)),
                          pl.BlockSpec((WINDOW, A_KV_WIDTH), prev(1))],
                out_specs=[pl.BlockSpec((WINDOW, A_WIDTH), lambda s: (s, 0))],
                out_shape=[jax.ShapeDtypeStruct((n_steps * WINDOW, A_WIDTH), BF16)])


def _swa_sample_block(sink_ref, q_ref, z_ref, kn_ref, vn_ref, ck_ref, cv_ref, o_ref, ko_ref, vo_ref, *, n_seq, t_new):
    W = WINDOW
    hd = A_HEAD_DIM
    T = t_new
    R = A_GROUP * T
    S = 2 * W
    t_idx = lax.broadcasted_iota(jnp.int32, (R, S), 0) % T
    j_idx = lax.broadcasted_iota(jnp.int32, (R, S), 1)
    mask = ((j_idx < W) & (j_idx > t_idx)) | ((j_idx >= W) & (j_idx - W <= t_idx))
    nt = (((1,), (1,)), ((), ()))
    lane = lax.broadcasted_iota(jnp.int32, (A_KV_WIDTH, W), 1)
    pad = jnp.zeros((W - T, A_KV_WIDTH), F32)
    q_all = q_ref[...].astype(F32)
    kn_all = kn_ref[...].astype(F32)
    vn_all = vn_ref[...].astype(F32)
    jobs = []
    for s in range(n_seq):
        rows = slice(s * T, (s + 1) * T)
        cat = []
        for c_ref, new, out_ref in ((ck_ref, kn_all, ko_ref), (cv_ref, vn_all, vo_ref)):
            old = c_ref[s]
            fresh = jnp.concatenate([new[rows, :], pad], axis=0).T
            out_ref[s] = pltpu.roll(jnp.where(lane < T, fresh, old), W - T, 1)
            cat.append(jnp.concatenate([old, fresh], axis=1).astype(BF16))
        for kh in range(A_KV_HEADS):
            heads = [kh * A_GROUP + g for g in range(A_GROUP)]
            q4 = jnp.concatenate([q_all[rows, h * hd:(h + 1) * hd] for h in heads], axis=0) * (hd ** -0.5)
            kv = slice(kh * hd, (kh + 1) * hd)
            jobs.append(dict(heads=heads, values=cat[1][kv, :],
                             sc=jnp.dot(q4.astype(BF16), cat[0][kv, :], preferred_element_type=F32)))
    yield

    for job in jobs:
        sink = jnp.concatenate([jnp.full((T, 1), sink_ref[h], F32) for h in job["heads"]], axis=0)
        sc = jnp.where(mask, job["sc"], -jnp.inf)
        mx = jnp.maximum(sink, jnp.max(sc, axis=1, keepdims=True))
        p = jnp.exp(sc - mx)
        den = jnp.exp(sink - mx) + jnp.sum(p, axis=1, keepdims=True)
        job["o4"] = lax.dot_general(p.astype(BF16), job["values"], nt, preferred_element_type=F32) / den
    yield

    att_rows = []
    for s in range(n_seq):
        outs = []
        for job in jobs[s * A_KV_HEADS:(s + 1) * A_KV_HEADS]:
            outs += [job["o4"][g * T:(g + 1) * T, :] for g in range(A_GROUP)]
        att_rows.append(jnp.concatenate(outs, axis=1))
    att = att_rows[0] if n_seq == 1 else jnp.concatenate(att_rows, axis=0)
    o_ref[...] = (att * z_ref[...].astype(F32)).astype(o_ref.dtype)


def _swa_sample_side(pa_qz, pa_kv, cache_k, cache_v, sinks, *, n_steps, n_batch, t_new, row0):
    n_seq = n_batch // n_steps
    rows = n_seq * t_new
    blk0 = row0 // rows
    cache_spec = pl.BlockSpec((n_seq, A_KV_WIDTH, WINDOW), lambda s: (s, 0, 0))

    def fn(step, in_refs, out_refs):
        return _swa_sample_block(*in_refs, *out_refs, n_seq=n_seq, t_new=t_new)

    return dict(fn=fn, args=[sinks, pa_qz, pa_qz, pa_kv, pa_kv, cache_k, cache_v],
                in_specs=[pl.BlockSpec(memory_space=pltpu.SMEM),
                          pl.BlockSpec((rows, A_WIDTH), lambda s: (blk0 + s, 0)),
                          pl.BlockSpec((rows, A_WIDTH), lambda s: (blk0 + s, 1)),
                          pl.BlockSpec((rows, A_KV_WIDTH), lambda s: (blk0 + s, 0)),
                          pl.BlockSpec((rows, A_KV_WIDTH), lambda s: (blk0 + s, 1)),
                          cache_spec, cache_spec],
                out_specs=[pl.BlockSpec((rows, A_WIDTH), lambda s: (s, 0)), cache_spec, cache_spec],
                out_shape=[jax.ShapeDtypeStruct((n_batch * t_new, A_WIDTH), BF16),
                           jax.ShapeDtypeStruct(cache_k.shape, F32),
                           jax.ShapeDtypeStruct(cache_v.shape, F32)])


def _merge_tile(mm_ref, ma_ref, wbm_ref, wba_ref, gm_ref, ga_ref, o_ref):
    bm = jnp.dot(mm_ref[...], wbm_ref[...], preferred_element_type=F32)
    yield
    ba = jnp.dot(ma_ref[...], wba_ref[...], preferred_element_type=F32)
    yield
    o_ref[...] = (gm_ref[...].astype(F32) * bm + ga_ref[...].astype(F32) * ba).astype(o_ref.dtype)


def _merge(mix_m, mix_a, w_bm, w_ba, gates, *, m_rows, tn, row0, side=None, name):
    tm = min(1024, m_rows)
    nj = D_MODEL // tn
    blk0 = row0 // tm

    def fn(step, in_refs, out_refs):
        return _merge_tile(*in_refs, *out_refs)

    main = dict(fn=fn, args=[mix_m, mix_a, w_bm, w_ba, gates, gates],
                in_specs=[pl.BlockSpec((tm, M_WIDTH), lambda s: (s // nj, 0)),
                          pl.BlockSpec((tm, A_WIDTH), lambda s: (s // nj, 0)),
                          pl.BlockSpec((M_WIDTH, tn), lambda s: (0, s % nj)),
                          pl.BlockSpec((A_WIDTH, tn), lambda s: (0, s % nj)),
                          pl.BlockSpec((tm, tn), lambda s: (blk0 + s // nj, s % nj)),
                          pl.BlockSpec((tm, tn), lambda s: (blk0 + s // nj, s % nj + nj))],
                out_specs=[pl.BlockSpec((tm, tn), lambda s: (s // nj, s % nj))],
                out_shape=[jax.ShapeDtypeStruct((m_rows, D_MODEL), BF16)])
    (out,), side_outs = _hosted_call((m_rows // tm) * nj, main, side, name)
    return out, side_outs


def _out_tile(j, mg_ref, w_ref, x_ref, g_ref, b_ref, o_ref, *, tn):
    nj = o_ref.shape[1] // tn
    o_ref[:, pl.ds(pl.multiple_of(j * tn, tn), tn)] = (
        DEEPNORM_ALPHA * x_ref[...] + jnp.dot(mg_ref[...], w_ref[...], preferred_element_type=F32))
    yield

    @pl.when(j == nj - 1)
    def _():
        chunk = 8 * SUBLANES

        def norm_rows(r, carry):
            rows = pl.ds(pl.multiple_of(r * chunk, chunk), chunk)
            y = o_ref[rows, :]
            yc = y - jnp.mean(y, axis=1, keepdims=True)
            var = jnp.mean(jnp.square(yc), axis=1, keepdims=True)
            o_ref[rows, :] = yc * lax.rsqrt(var + LN_EPS) * g_ref[...] + b_ref[...]
            return carry

        lax.fori_loop(0, o_ref.shape[0] // chunk, norm_rows, 0)


def _out_proj(merged, w_out, x2d, ln_g, ln_b, *, tm, tn, side=None, name):
    m_rows = merged.shape[0]
    nj = D_MODEL // tn

    def fn(step, in_refs, out_refs):
        return _out_tile(step % nj, *in_refs, *out_refs, tn=tn)

    main = dict(fn=fn, args=[merged, w_out, x2d, ln_g, ln_b],
                in_specs=[pl.BlockSpec((tm, D_MODEL), lambda s: (s // nj, 0)),
                          pl.BlockSpec((D_MODEL, tn), lambda s: (0, s % nj)),
                          pl.BlockSpec((tm, tn), lambda s: (s // nj, s % nj)),
                          pl.BlockSpec((1, D_MODEL), lambda s: (0, 0)),
                          pl.BlockSpec((1, D_MODEL), lambda s: (0, 0))],
                out_specs=[pl.BlockSpec((tm, D_MODEL), lambda s: (s // nj, 0))],
                out_shape=[jax.ShapeDtypeStruct((m_rows, D_MODEL), F32)])
    (out,), side_outs = _hosted_call((m_rows // tm) * nj, main, side, name)
    return out, side_outs


def _rope_tables(positions):
    half = A_HEAD_DIM // 2
    lane = jnp.arange(LANES)
    inv = ROPE_THETA ** (-(lane % half).astype(F32) / half)
    ang = positions.astype(F32)[:, None] * inv[None, :]
    sign = jnp.where((lane % A_HEAD_DIM) < half, -1.0, 1.0).astype(F32)
    return jnp.cos(ang), jnp.sin(ang) * sign[None, :]


def kernel(x_prompt, x_sample, state_C, state_n, state_m, cache_k, cache_v, w_in, b_if, norm_m_g,
           attn_sinks, w_bm, w_ba, w_out, ln_g, ln_b):
    bp, seq, _ = x_prompt.shape
    bs, t_new, _ = x_sample.shape
    w_buf = cache_k.shape[2]
    assert DEPTH == 1 and w_buf == WINDOW and seq % WINDOW == 0
    n_p, n_s = bp * seq, bs * t_new

    wt = jnp.swapaxes(w_in[0], 0, 1)
    w_attn = _repack_attn(wt)
    bif = jnp.pad(b_if[0], (0, LANES - 2 * M_HEADS)).reshape(1, LANES)
    ng = norm_m_g[0].reshape(1, M_WIDTH)
    sinks = attn_sinks[0]
    lng, lnb = ln_g[0].reshape(1, D_MODEL), ln_b[0].reshape(1, D_MODEL)

    xp2 = x_prompt.reshape(n_p, D_MODEL)
    xs2 = x_sample.reshape(n_s, D_MODEL)
    cos_p, sin_p = _rope_tables(jnp.arange(seq))
    cos_s, sin_s = _rope_tables(PAST_LEN + jnp.arange(t_new))
    cos = jnp.concatenate([jnp.tile(cos_p, (bp, 1)), jnp.tile(cos_s, (bs, 1))], axis=0)
    sin = jnp.concatenate([jnp.tile(sin_p, (bp, 1)), jnp.tile(sin_s, (bs, 1))], axis=0)
    tn = PROJ_TN
    n_m = (n_p + n_s) // PROJ_TM
    blk_gate = MLSTM_COLS // tn
    blk_if = ATTN_COLS // LANES
    xb, pa_kv = _project_first(xp2, xs2, w_attn, cos, sin, tm=PROJ_TM // 4, tn=tn, name="proj_kv_cast")
    attn_tm = PROJ_TM // 2
    qz_tiles = 2 * A_WIDTH // tn
    repack_side = _repack_rest_side(wt, n_steps=qz_tiles * ((n_p + n_s) // attn_tm))
    cast_side = _cast_rows_side([w_bm[0], w_ba[0], w_out[0]], n_slabs=64)
    pa_qz, (w_rest, wbm, wba, wo) = _project(xb, w_attn, 2 * A_KV_WIDTH // tn, qz_tiles, tn=tn, tm=attn_tm,
                                             out_dtype=BF16, mode="attn", cos=cos, sin=sin,
                                             side=_join_sides(repack_side, cast_side), name="proj_qz_repack")
    tn_m = MLSTM_TN
    swa_side = _swa_prompt_side(pa_qz, pa_kv, sinks, n_steps=(MLSTM_COLS // tn_m) * n_m, n_batch=bp, seq=seq)
    if_side = _gate_proj_side(xb, w_attn, blk_if, n_steps=(MLSTM_COLS // tn_m) * n_m)
    pm, (mix_a_p, g_if) = _project(xb, w_rest, 0, MLSTM_COLS // tn_m, tn=tn_m, out_dtype=BF16, mode="mlstm",
                                   side=_join_sides(swa_side, if_side), name="proj_mlstm_swa_p")
    mlstm_side = _mlstm_prompt_side(pm, g_if, bif, ng, n_steps=(GATE_COLS // tn) * n_m, seq=seq, chunk=128)
    gates, (mix_m_p, p_c, p_n, p_m) = _project(xb, w_rest, blk_gate, GATE_COLS // tn, tn=tn, out_dtype=BF16,
                                               mode="sigmoid", side=mlstm_side, name="proj_gate_mlstm_p")

    init = (state_C[0], state_n[0], jnp.broadcast_to(state_m[0][:, :, None], (bs, M_HEADS, LANES)))
    merge_tn, out_tm, out_tn = 512, 512, 1024
    mlstm_s_side = _mlstm_sample_side(pm, g_if, bif, ng, init, n_steps=(n_p // 1024) * (D_MODEL // merge_tn),
                                      n_batch=bs, t_new=t_new, row0=n_p)
    def window_minor(c):
        return jnp.transpose(c, (0, 2, 3, 1)).reshape(bs, A_KV_WIDTH, w_buf)

    def window_major(c):
        return jnp.transpose(c.reshape(bs, A_KV_HEADS, A_HEAD_DIM, w_buf), (0, 3, 1, 2))[None]

    swa_s_side = _swa_sample_side(pa_qz, pa_kv, window_minor(cache_k[0]), window_minor(cache_v[0]), sinks,
                                  n_steps=(n_p // out_tm) * (D_MODEL // out_tn), n_batch=bs, t_new=t_new, row0=n_p)

    merged, (mix_m_s, s_c, s_n, s_m) = _merge(mix_m_p, mix_a_p, wbm, wba, gates, m_rows=n_p, tn=merge_tn, row0=0,
                                              side=mlstm_s_side, name="merge_p_mlstm_s")
    y_p, (mix_a_s, s_k, s_v) = _out_proj(merged, wo, xp2, lng, lnb, tm=out_tm, tn=out_tn, side=swa_s_side,
                                         name="out_p_swa_s")
    y_p = y_p.reshape(bp, seq, D_MODEL)
    kv_last = jnp.stack([pa_kv[(b + 1) * seq - w_buf:(b + 1) * seq] for b in range(bp)]).astype(F32)
    p_k = kv_last[..., :A_KV_WIDTH].reshape(1, bp, w_buf, A_KV_HEADS, A_HEAD_DIM)
    p_v = kv_last[..., A_KV_WIDTH:].reshape(1, bp, w_buf, A_KV_HEADS, A_HEAD_DIM)
    p_c, p_n, p_m = p_c[:bp], p_n[:bp], p_m[:bp]

    merged, _ = _merge(mix_m_s, mix_a_s, wbm, wba, gates, m_rows=n_s, tn=merge_tn, row0=n_p, name="merge_s")
    y_s, _ = _out_proj(merged, wo, xs2, lng, lnb, tm=out_tm, tn=out_tn, name="out_s")
    y_s = y_s.reshape(bs, t_new, D_MODEL)

    def st(a, like):
        return a[None].astype(like.dtype)

    return (y_p, y_s,
            st(p_c, state_C), st(p_n, state_n), st(p_m[:, :, 0], state_m), p_k, p_v,
            st(s_c, state_C), st(s_n, state_n), st(s_m[:, :, 0], state_m),
            window_major(s_k), window_major(s_v))
```

```python
import functools

import jax
import jax.numpy as jnp
from jax import lax
from jax.experimental import pallas as pl
from jax.experimental.pallas import tpu as pltpu

F32 = jnp.float32
BF16 = jnp.bfloat16

D_MODEL = 4096
DEPTH = 1
PAST_LEN = 8192
M_HEADS = 8
M_WIDTH = D_MODEL // 2
M_HEAD_DIM = M_WIDTH // M_HEADS
A_HEAD_DIM = 64
A_WIDTH = D_MODEL // 2
A_HEADS = A_WIDTH // A_HEAD_DIM
A_KV_HEADS = A_HEADS // 4
A_GROUP = A_HEADS // A_KV_HEADS
A_KV_WIDTH = A_KV_HEADS * A_HEAD_DIM
WINDOW = 128
ROPE_THETA = 10000.0
LN_EPS = 1e-5
DEEPNORM_ALPHA = (2.0 * DEPTH) ** 0.25

LANES = 128
SUBLANES = 8
VMEM_LIMIT = 60 * 1024 * 1024

_OFF_IF = 5 * M_WIDTH
_OFF_QA = _OFF_IF + 2 * M_HEADS

PROJ_TM = 1024
PROJ_TN = 1024
MLSTM_TN = 1280
REPACK_TN = 512
REPACK_ROWS = 2048
MLSTM_COLS = 5 * M_WIDTH
ATTN_COLS = 2 * A_WIDTH + 2 * A_KV_WIDTH
GATE_COLS = 2 * D_MODEL
MAIN_COLS = MLSTM_COLS + ATTN_COLS + GATE_COLS


def _params(sem):
    return pltpu.CompilerParams(dimension_semantics=sem, vmem_limit_bytes=VMEM_LIMIT)


def _repack_src_block(t):
    n_main = MAIN_COLS // REPACK_TN
    za0 = (MLSTM_COLS + A_WIDTH) // REPACK_TN
    ka0 = za0 + A_WIDTH // REPACK_TN
    g0 = ka0 + 2 * A_KV_WIDTH // REPACK_TN
    kv_blocks = 2 * A_KV_WIDTH // REPACK_TN
    za_blocks = A_WIDTH // REPACK_TN
    moved = t + jnp.where((t >= za0) & (t < ka0), kv_blocks, 0) - jnp.where((t >= ka0) & (t < g0), za_blocks, 0)
    return jnp.where(t == n_main, _OFF_IF // REPACK_TN, moved)


def _repack_block(t, a_ref, b_ref, o_ref):
    n_aligned = _OFF_IF // REPACK_TN
    n_main = MAIN_COLS // REPACK_TN
    sh = _OFF_QA - _OFF_IF
    a = a_ref[...]
    moved = jnp.concatenate([a[sh:, :], b_ref[...]], axis=0)
    row = lax.broadcasted_iota(jnp.int32, a.shape, 0)
    gate_rows = jnp.where(row < sh, a, 0.0)
    src = jnp.where(t < n_aligned, a, jnp.where(t < n_main, moved, gate_rows))
    o_ref[...] = src.T.astype(o_ref.dtype)


_ATTN_TILE0 = MLSTM_COLS // REPACK_TN
_ATTN_TILES = ATTN_COLS // REPACK_TN
_FIRST_TILES = _ATTN_TILES + 1
_REST_TILES = MAIN_COLS // REPACK_TN - _ATTN_TILES
_ROW_HALVES = D_MODEL // REPACK_ROWS
_KV_TILES = 2 * A_KV_WIDTH // REPACK_TN


def _repack_specs(tile_of, half_of, out_tile_of):
    per = REPACK_TN // (_OFF_QA - _OFF_IF)
    return dict(
        in_specs=[pl.BlockSpec((REPACK_TN, REPACK_ROWS), lambda *g: (_repack_src_block(tile_of(*g)), half_of(*g))),
                  pl.BlockSpec((_OFF_QA - _OFF_IF, REPACK_ROWS),
                               lambda *g: ((_repack_src_block(tile_of(*g)) + 1) * per, half_of(*g)))],
        out_specs=[pl.BlockSpec((REPACK_ROWS, REPACK_TN), lambda *g: (half_of(*g), out_tile_of(*g)))])


def _repack_attn(wt):
    def tile_of(u, r=None):
        kv0 = _ATTN_TILE0 + _ATTN_TILES - _KV_TILES
        return jnp.where(u < _KV_TILES, kv0 + u,
                         jnp.where(u < _ATTN_TILES, _ATTN_TILE0 + u - _KV_TILES, MAIN_COLS // REPACK_TN))

    specs = _repack_specs(tile_of, lambda u, r: r, lambda u, r: u)

    def body(a_ref, b_ref, o_ref):
        _repack_block(tile_of(pl.program_id(0)), a_ref, b_ref, o_ref)

    return pl.pallas_call(
        body,
        grid=(_FIRST_TILES, _ROW_HALVES),
        in_specs=specs["in_specs"],
        out_specs=specs["out_specs"][0],
        out_shape=jax.ShapeDtypeStruct((D_MODEL, _FIRST_TILES * REPACK_TN), BF16),
        compiler_params=_params(("parallel", "parallel")),
        name="repack_attn",
    )(wt, wt)


def _repack_rest_side(wt, *, n_steps):
    n_blocks = _REST_TILES * _ROW_HALVES
    assert n_steps >= n_blocks

    def local(s):
        return jnp.minimum(s, n_blocks - 1) // _ROW_HALVES

    def tile_of(s):
        return local(s) + jnp.where(local(s) >= _ATTN_TILE0, _ATTN_TILES, 0)

    def half_of(s):
        return jnp.minimum(s, n_blocks - 1) % _ROW_HALVES

    def fn(step, in_refs, out_refs):
        _repack_block(tile_of(step), *in_refs, *out_refs)
        yield

    return dict(fn=fn, args=[wt, wt], out_shape=[jax.ShapeDtypeStruct((D_MODEL, _REST_TILES * REPACK_TN), BF16)],
                **_repack_specs(tile_of, half_of, local))


def _hosted_body(*refs, n_main_in, n_side_in, n_main_out, main_fn, side_fn):
    main_in = refs[:n_main_in]
    side_in = refs[n_main_in:n_main_in + n_side_in]
    main_out = refs[n_main_in + n_side_in:n_main_in + n_side_in + n_main_out]
    side_out = refs[n_main_in + n_side_in + n_main_out:]
    step = pl.program_id(0)
    stages = [fn(step, i, o) for fn, i, o in ((side_fn, side_in, side_out), (main_fn, main_in, main_out)) if fn]
    while stages:
        alive = []
        for g in stages:
            try:
                next(g)
                alive.append(g)
            except StopIteration:
                pass
        stages = alive


def _join_sides(*sides):
    def fn(step, in_refs, out_refs):
        gens, i0, o0 = [], 0, 0
        for sd in sides:
            ni, no = len(sd["args"]), len(sd["out_shape"])
            gens.append(sd["fn"](step, in_refs[i0:i0 + ni], out_refs[o0:o0 + no]))
            i0, o0 = i0 + ni, o0 + no
        while gens:
            alive = []
            for g in gens:
                try:
                    next(g)
                    alive.append(g)
                except StopIteration:
                    pass
            gens = alive
            if gens:
                yield

    return dict(fn=fn, **{k: sum((sd[k] for sd in sides), []) for k in ("args", "in_specs", "out_specs", "out_shape")})


def _cast_rows_side(mats, *, n_slabs):
    def fn(step, in_refs, out_refs):
        for src, dst in zip(in_refs, out_refs):
            dst[...] = src[...].astype(dst.dtype)
        yield

    specs = [pl.BlockSpec((m.shape[0] // n_slabs, m.shape[1]), lambda s: (jnp.minimum(s, n_slabs - 1), 0))
             for m in mats]
    return dict(fn=fn, args=list(mats), in_specs=specs, out_specs=list(specs),
                out_shape=[jax.ShapeDtypeStruct(m.shape, BF16) for m in mats])


def _gate_proj_side(xb, w, col_block, *, n_steps):
    rows = xb.shape[0] // n_steps
    assert rows * n_steps == xb.shape[0] and rows % SUBLANES == 0

    def fn(step, in_refs, out_refs):
        out_refs[0][...] = jnp.dot(in_refs[0][...], in_refs[1][...], preferred_element_type=F32)
        yield

    return dict(fn=fn, args=[xb, w],
                in_specs=[pl.BlockSpec((rows, D_MODEL), lambda s: (s, 0)),
                          pl.BlockSpec((D_MODEL, LANES), lambda s: (0, col_block))],
                out_specs=[pl.BlockSpec((rows, LANES), lambda s: (s, 0))],
                out_shape=[jax.ShapeDtypeStruct((xb.shape[0], LANES), F32)])


def _hosted_call(n_steps, main, side, name):
    side = side or dict(fn=None, args=[], in_specs=[], out_specs=[], out_shape=[])
    outs = pl.pallas_call(
        functools.partial(_hosted_body, n_main_in=len(main["args"]), n_side_in=len(side["args"]),
                          n_main_out=len(main["out_shape"]), main_fn=main["fn"], side_fn=side["fn"]),
        grid=(n_steps,),
        in_specs=main["in_specs"] + side["in_specs"],
        out_specs=main["out_specs"] + side["out_specs"],
        out_shape=main["out_shape"] + side["out_shape"],
        compiler_params=_params(("arbitrary",)),
        name=name,
    )(*main["args"], *side["args"])
    n_main = len(main["out_shape"])
    return outs[:n_main], outs[n_main:]


def _rope_slab(slab, cos, sin, lo):
    partner = jnp.where(lo, pltpu.roll(slab, LANES - 32, 1), pltpu.roll(slab, 32, 1))
    return slab * cos + partner * sin


def _proj_tile(j, x_ref, w_ref, rest, mode):
    tm, tn = x_ref.shape[0], w_ref.shape[1]
    pw = next((c for c in (4 * LANES, 2 * LANES) if tn % c == 0), tn)
    n_parts = tn // pw
    o_ref = rest[-1]

    if mode == "plain":
        def epilogue(acc, col0):
            return acc
    elif mode == "sigmoid":
        def epilogue(acc, col0):
            return jax.nn.sigmoid(acc)
    elif mode == "mlstm":
        per = M_WIDTH // LANES

        def epilogue(acc, col0):
            sig = jax.nn.sigmoid(acc)
            outs = []
            for c in range(acc.shape[1] // LANES):
                slab = j * (tn // LANES) + col0 // LANES + c
                a, s = acc[:, c * LANES:(c + 1) * LANES], sig[:, c * LANES:(c + 1) * LANES]
                outs.append(jnp.where(slab < 3 * per, a, jnp.where(slab < 4 * per, s, a * s)))
            return jnp.concatenate(outs, axis=1)
    else:
        cos_ref, sin_ref, _ = rest
        n_q = A_WIDTH // tn
        n_plain = n_q + A_WIDTH // tn
        rope_w = A_KV_WIDTH
        lo = (lax.broadcasted_iota(jnp.int32, (tm, LANES), 1) % A_HEAD_DIM) < (A_HEAD_DIM // 2)

        def epilogue(acc, col0):
            outs = []
            for c in range(acc.shape[1] // LANES):
                slab = acc[:, c * LANES:(c + 1) * LANES]
                roped = _rope_slab(slab, cos_ref[...], sin_ref[...], lo)
                use_rope = (j < n_q) | (j >= n_plain) if col0 + c * LANES < rope_w else j < n_q
                outs.append(jnp.where(use_rope, roped, jnp.where(j < n_plain, slab * jax.nn.sigmoid(slab), slab)))
            return jnp.concatenate(outs, axis=1)

    for p in range(n_parts):
        acc = jnp.dot(x_ref[...], w_ref[:, p * pw:(p + 1) * pw], preferred_element_type=F32)
        o_ref[:, p * pw:(p + 1) * pw] = epilogue(acc, p * pw).astype(o_ref.dtype)
        yield


def _project(xb, w, col_block0, n_blocks, *, tn, out_dtype, mode, tm=PROJ_TM, cos=None, sin=None, side=None,
             name):
    m_rows = xb.shape[0]
    tm = min(tm, m_rows)
    n_m = m_rows // tm
    in_specs = [pl.BlockSpec((tm, D_MODEL), lambda s: (s % n_m, 0)),
                pl.BlockSpec((D_MODEL, tn), lambda s: (0, s // n_m + col_block0))]
    args = [xb, w]
    if mode == "attn":
        in_specs += [pl.BlockSpec((tm, LANES), lambda s: (s % n_m, 0))] * 2
        args += [cos, sin]

    def fn(step, in_refs, out_refs):
        return _proj_tile(step // n_m, in_refs[0], in_refs[1], tuple(in_refs[2:]) + tuple(out_refs), mode)

    main = dict(fn=fn, args=args, in_specs=in_specs,
                out_specs=[pl.BlockSpec((tm, tn), lambda s: (s % n_m, s // n_m))],
                out_shape=[jax.ShapeDtypeStruct((m_rows, n_blocks * tn), out_dtype)])
    (out,), side_outs = _hosted_call(n_blocks * n_m, main, side, name)
    return out, side_outs


def _project_first(xp2, xs2, w, cos, sin, *, tm, tn, name):
    n_p, n_s = xp2.shape[0] // tm, xs2.shape[0] // tm
    kv_tile = 2 * (A_WIDTH // tn)

    def body(xp_ref, xs_ref, w_ref, cos_ref, sin_ref, xb_ref, o_ref):
        i = pl.program_id(0)
        xb_ref[...] = jnp.where(i < n_p, xp_ref[...], xs_ref[...]).astype(xb_ref.dtype)
        for _ in _proj_tile(kv_tile, xb_ref, w_ref, (cos_ref, sin_ref, o_ref), "attn"):
            pass

    return pl.pallas_call(
        body,
        grid=(n_p + n_s,),
        in_specs=[pl.BlockSpec((tm, D_MODEL), lambda i: (jnp.minimum(i, n_p - 1), 0)),
                  pl.BlockSpec((tm, D_MODEL), lambda i: (jnp.maximum(i - n_p, 0), 0)),
                  pl.BlockSpec((D_MODEL, tn), lambda i: (0, 0)),
                  pl.BlockSpec((tm, LANES), lambda i: (i, 0)),
                  pl.BlockSpec((tm, LANES), lambda i: (i, 0))],
        out_specs=[pl.BlockSpec((tm, D_MODEL), lambda i: (i, 0)),
                   pl.BlockSpec((tm, tn), lambda i: (i, 0))],
        out_shape=[jax.ShapeDtypeStruct(((n_p + n_s) * tm, D_MODEL), BF16),
                   jax.ShapeDtypeStruct(((n_p + n_s) * tm, tn), BF16)],
        compiler_params=_params(("arbitrary",)),
        name=name,
    )(xp2, xs2, w, cos, sin)


def _log_sigmoid(x):
    return jnp.minimum(x, 0.0) - jnp.log(1.0 + jnp.exp(-jnp.abs(x)))


def _cumsum_rows(a):
    n = a.shape[0]
    row = lax.broadcasted_iota(jnp.int32, a.shape, 0)
    shift = 1
    while shift < n:
        a = a + jnp.where(row >= shift, pltpu.roll(a, shift, 0), 0.0)
        shift *= 2
    return a


def _mlstm_chunk(first_chunk, refs, *, chunk, n_seq, has_init):
    q_ref, k_ref, v_ref, o_ref, z_ref, g_ref, bif_ref, ng_ref = refs[:8]
    if has_init:
        c0_ref, n0_ref, m0_ref = refs[8:11]
        mix_ref, c_ref, n_ref, m_ref = refs[11:]
    else:
        mix_ref, c_ref, n_ref, m_ref = refs[8:]
        c0_ref, n0_ref, m0_ref = c_ref, n_ref, m_ref

        @pl.when(first_chunk)
        def _():
            c_ref[...] = jnp.zeros_like(c_ref)
            n_ref[...] = jnp.zeros_like(n_ref)
            m_ref[...] = jnp.zeros_like(m_ref)

    L = chunk
    dh = M_HEAD_DIM
    lane = lax.broadcasted_iota(jnp.int32, (L, LANES), 1)
    causal = (lax.broadcasted_iota(jnp.int32, (L, L), 1) <= lax.broadcasted_iota(jnp.int32, (L, L), 0))
    pad_rows = (-L) % LANES

    gate_cols, gate_rows = [], []
    for s in range(n_seq):
        g = g_ref[s * L:(s + 1) * L, :] + bif_ref[...]
        b = _cumsum_rows(jnp.where(lane >= M_HEADS, _log_sigmoid(g), 0.0))
        col = jnp.where(lane < M_HEADS, g, b)
        colp = jnp.concatenate([col, jnp.zeros((pad_rows, LANES), F32)], axis=0) if pad_rows else col
        gate_cols.append(col)
        gate_rows.append(colp.T)

    def per_seq(ref, cols):
        a = ref[:, cols]
        if n_seq == 1:
            return [a]
        a = a.astype(F32)
        return [a[s * L:(s + 1) * L] for s in range(n_seq)]

    jobs = []
    for h in range(M_HEADS):
        cols = slice(h * dh, (h + 1) * dh)
        qs, ks, vs = per_seq(q_ref, cols), per_seq(k_ref, cols), per_seq(v_ref, cols)
        for s in range(n_seq):
            q = qs[s]
            k = ks[s] * (dh ** -0.5)
            qb, kb, vb = q.astype(BF16), k.astype(BF16), vs[s].astype(BF16)
            i_c = gate_cols[s][:, h:h + 1]
            b_c = gate_cols[s][:, M_HEADS + h:M_HEADS + h + 1]
            i_r = gate_rows[s][h:h + 1, :L]
            b_r = gate_rows[s][M_HEADS + h:M_HEADS + h + 1, :L]
            m_prev = m0_ref[s, h:h + 1, 0:1]

            log_d = jnp.where(causal, (b_c - b_r) + i_r, -jnp.inf)
            a_c = b_c + m_prev
            m_t = jnp.maximum(a_c, jnp.max(log_d, axis=1, keepdims=True))
            m_new = m_t[L - 1:L, :]
            b_last = b_c[L - 1:L, :]
            w_c = jnp.exp((b_last - b_c) + i_c - m_new)
            kw = k.astype(F32) * w_c
            jobs.append(dict(
                h=h, s=s, q=q, vb=vb, log_d=log_d, a_c=a_c, m_t=m_t, m_new=m_new, kw=kw,
                decay=jnp.exp(b_last + m_prev - m_new),
                qk=lax.dot_general(qb, kb, (((1,), (1,)), ((), ())), preferred_element_type=F32),
                qc=jnp.dot(qb, c0_ref[s, h].astype(BF16), preferred_element_type=F32)))
    yield

    for job in jobs:
        job["sc"] = job["qk"] * jnp.exp(job["log_d"] - job["m_t"])
        job["sv"] = jnp.dot(job["sc"].astype(BF16), job["vb"], preferred_element_type=F32)
    yield

    for h in range(M_HEADS):
        cols = slice(h * dh, (h + 1) * dh)
        os_, zs = per_seq(o_ref, cols), per_seq(z_ref, cols)
        mix_parts = []
        for s in range(n_seq):
            job = jobs[h * n_seq + s]
            n_prev = n0_ref[s, h:h + 1, :]
            inter = jnp.exp(job["a_c"] - job["m_t"])
            num = job["sv"] + inter * job["qc"]
            qn = jnp.sum(job["q"].astype(F32) * n_prev, axis=1, keepdims=True)
            den = jnp.sum(job["sc"], axis=1, keepdims=True) + inter * qn
            hid = num / jnp.maximum(jnp.abs(den), jnp.exp(-job["m_t"]))

            c_ref[s, h] = job["decay"] * c0_ref[s, h] + lax.dot_general(
                job["kw"].astype(BF16), job["vb"], (((0,), (0,)), ((), ())), preferred_element_type=F32)
            n_ref[s, h:h + 1, :] = job["decay"] * n_prev + jnp.sum(job["kw"], axis=0, keepdims=True)
            m_ref[s, h:h + 1, :] = jnp.broadcast_to(job["m_new"], (1, LANES))

            hid = os_[s].astype(F32) * hid
            mu = jnp.mean(hid, axis=1, keepdims=True)
            var = jnp.mean(jnp.square(hid - mu), axis=1, keepdims=True)
            hid = (hid - mu) * lax.rsqrt(var + LN_EPS) * ng_ref[:, cols]
            mix_parts.append(hid * zs[s].astype(F32))
        mix = mix_parts[0] if n_seq == 1 else jnp.concatenate(mix_parts, axis=0)
        mix_ref[:, cols] = mix.astype(mix_ref.dtype)


def _mlstm_state_shapes(n_slots):
    return [jax.ShapeDtypeStruct((n_slots, M_HEADS, M_HEAD_DIM, M_HEAD_DIM), F32),
            jax.ShapeDtypeStruct((n_slots, M_HEADS, M_HEAD_DIM), F32),
            jax.ShapeDtypeStruct((n_slots, M_HEADS, LANES), F32)]


def _mlstm_prompt_side(pm, gates, bif, ng, *, n_steps, seq, chunk):
    n_chunks = seq // chunk
    n_slots = -(-n_steps // n_chunks)
    assert n_steps * chunk <= pm.shape[0]
    st_specs = [pl.BlockSpec((1, M_HEADS, M_HEAD_DIM, M_HEAD_DIM), lambda s: (s // n_chunks, 0, 0, 0)),
                pl.BlockSpec((1, M_HEADS, M_HEAD_DIM), lambda s: (s // n_chunks, 0, 0)),
                pl.BlockSpec((1, M_HEADS, LANES), lambda s: (s // n_chunks, 0, 0))]

    def fn(step, in_refs, out_refs):
        return _mlstm_chunk(step % n_chunks == 0, tuple(in_refs) + tuple(out_refs), chunk=chunk, n_seq=1,
                            has_init=False)

    def grp(g):
        return pl.BlockSpec((chunk, M_WIDTH), lambda s: (s, g))

    return dict(fn=fn, args=[pm, pm, pm, pm, pm, gates, bif, ng],
                in_specs=[grp(g) for g in range(5)] + [pl.BlockSpec((chunk, LANES), lambda s: (s, 0)),
                                                       pl.BlockSpec((1, LANES), lambda s: (0, 0)),
                                                       pl.BlockSpec((1, M_WIDTH), lambda s: (0, 0))],
                out_specs=[pl.BlockSpec((chunk, M_WIDTH), lambda s: (s, 0))] + st_specs,
                out_shape=[jax.ShapeDtypeStruct((n_steps * chunk, M_WIDTH), BF16)] + _mlstm_state_shapes(n_slots))


def _mlstm_sample_side(pm, gates, bif, ng, init, *, n_steps, n_batch, t_new, row0):
    n_seq = n_batch // n_steps
    rows = n_seq * t_new
    blk0 = row0 // rows
    in_specs = [pl.BlockSpec((rows, M_WIDTH), lambda s, g=g: (blk0 + s, g)) for g in range(5)]
    in_specs += [pl.BlockSpec((rows, LANES), lambda s: (blk0 + s, 0)),
                 pl.BlockSpec((1, LANES), lambda s: (0, 0)),
                 pl.BlockSpec((1, M_WIDTH), lambda s: (0, 0))]
    st_specs = [pl.BlockSpec((n_seq, M_HEADS, M_HEAD_DIM, M_HEAD_DIM), lambda s: (s, 0, 0, 0)),
                pl.BlockSpec((n_seq, M_HEADS, M_HEAD_DIM), lambda s: (s, 0, 0)),
                pl.BlockSpec((n_seq, M_HEADS, LANES), lambda s: (s, 0, 0))]

    def fn(step, in_refs, out_refs):
        return _mlstm_chunk(None, tuple(in_refs) + tuple(out_refs), chunk=t_new, n_seq=n_seq, has_init=True)

    return dict(fn=fn, args=[pm, pm, pm, pm, pm, gates, bif, ng, *init],
                in_specs=in_specs + st_specs,
                out_specs=[pl.BlockSpec((rows, M_WIDTH), lambda s: (s, 0))] + st_specs,
                out_shape=[jax.ShapeDtypeStruct((n_batch * t_new, M_WIDTH), BF16)] + _mlstm_state_shapes(n_batch))


def _swa_prompt_block(has_prev, sink_ref, q_ref, z_ref, kc_ref, kp_ref, vc_ref, vp_ref, o_ref):
    W = WINDOW
    hd = A_HEAD_DIM
    row = lax.broadcasted_iota(jnp.int32, (W, LANES), 0)
    col = lax.broadcasted_iota(jnp.int32, (W, LANES), 1)
    causal = col <= row
    prev_ok = jnp.logical_and(jnp.logical_not(causal), has_prev)
    lo = col < hd
    nt = (((1,), (1,)), ((), ()))
    slabs_per_kv = A_GROUP * hd // LANES

    def swap_halves(a):
        return pltpu.roll(a.astype(F32), hd, 1).astype(BF16)

    jobs = []
    for pair in range(A_KV_WIDTH // LANES):
        ks = slice(pair * LANES, (pair + 1) * LANES)
        kc, kp, vc, vp = kc_ref[:, ks], kp_ref[:, ks], vc_ref[:, ks], vp_ref[:, ks]
        kc_s, kp_s, vc_s, vp_s = swap_halves(kc), swap_halves(kp), swap_halves(vc), swap_halves(vp)
        zero = jnp.zeros_like(kc)
        for sub in range(2):
            at_lo = (kc, kp) if sub == 0 else (kc_s, kp_s)
            at_hi = (kc_s, kp_s) if sub == 0 else (kc, kp)
            keys = jnp.concatenate([jnp.where(lo, at_lo[0], zero), jnp.where(lo, at_lo[1], zero),
                                    jnp.where(lo, zero, at_hi[0]), jnp.where(lo, zero, at_hi[1])], axis=0)
            v_lo = jnp.concatenate([vc, vp] if sub == 0 else [vc_s, vp_s], axis=0)
            v_hi = jnp.concatenate([vc_s, vp_s] if sub == 0 else [vc, vp], axis=0)
            kvh = 2 * pair + sub
            slabs = [kvh * slabs_per_kv + half for half in range(slabs_per_kv)]
            qs = jnp.concatenate([q_ref[:, sl * LANES:(sl + 1) * LANES] for sl in slabs], axis=0) * (hd ** -0.5)
            jobs.append(dict(slabs=slabs, values=(v_lo, v_hi),
                             sc_all=lax.dot_general(qs, keys, nt, preferred_element_type=F32)))
    yield

    causal_n = jnp.concatenate([causal] * slabs_per_kv, axis=0)
    prev_ok_n = jnp.concatenate([prev_ok] * slabs_per_kv, axis=0)
    for job in jobs:
        outs = []
        for par in range(2):
            s_c = job["sc_all"][:, (2 * par) * W:(2 * par + 1) * W]
            s_p = job["sc_all"][:, (2 * par + 1) * W:(2 * par + 2) * W]
            sc = jnp.where(causal_n, s_c, jnp.where(prev_ok_n, s_p, -jnp.inf))
            sink = jnp.concatenate([jnp.full((W, 1), sink_ref[2 * sl + par], F32) for sl in job["slabs"]], axis=0)
            mx = jnp.maximum(jnp.max(sc, axis=1, keepdims=True), sink)
            p = jnp.exp(sc - mx)
            den = jnp.sum(p, axis=1, keepdims=True) + jnp.exp(sink - mx)
            pb = jnp.concatenate([jnp.where(causal_n, p, 0.0), jnp.where(causal_n, 0.0, p)], axis=1)
            outs.append((jnp.dot(pb.astype(BF16), job["values"][par], preferred_element_type=F32), den))
        job["outs"] = outs
    yield

    for job in jobs:
        (pv0, den0), (pv1, den1) = job["outs"]
        for n, sl in enumerate(job["slabs"]):
            rows, cs = slice(n * W, (n + 1) * W), slice(sl * LANES, (sl + 1) * LANES)
            att = jnp.where(lo, pv0[rows] / den0[rows], pv1[rows] / den1[rows])
            o_ref[:, cs] = (att * z_ref[:, cs].astype(F32)).astype(o_ref.dtype)


def _swa_prompt_side(pa_qz, pa_kv, sinks, *, n_steps, n_batch, seq):
    nb = seq // WINDOW

    def cur(col):
        return lambda s: (jnp.minimum(s // nb, n_batch - 1) * nb + s % nb, col)

    def prev(col):
        return lambda s: (jnp.minimum(s // nb, n_batch - 1) * nb + jnp.maximum(s % nb - 1, 0), col)

    def fn(step, in_refs, out_refs):
        return _swa_prompt_block(step % nb > 0, *in_refs, *out_refs)

    return dict(fn=fn, args=[sinks, pa_qz, pa_qz, pa_kv, pa_kv, pa_kv, pa_kv],
                in_specs=[pl.BlockSpec(memory_space=pltpu.SMEM),
                          pl.BlockSpec((WINDOW, A_WIDTH), cur(0)),
                          pl.BlockSpec((WINDOW, A_WIDTH), cur(1)),
                          pl.BlockSpec((WINDOW, A_KV_WIDTH), cur(0)),
                          pl.BlockSpec((WINDOW, A_KV_WIDTH), prev(0)),
                          pl.BlockSpec((WINDOW, A_KV_WIDTH), cur(1)),
                          pl.BlockSpec((WINDOW, A_KV_WIDTH), prev(1))],
                out_specs=[pl.BlockSpec((WINDOW, A_WIDTH), lambda s: (s, 0))],
                out_shape=[jax.ShapeDtypeStruct((n_steps * WINDOW, A_WIDTH), BF16)])


def _swa_sample_block(sink_ref, q_ref, z_ref, kn_ref, vn_ref, ck_ref, cv_ref, o_ref, ko_ref, vo_ref, *, n_seq, t_new):
    W = WINDOW
    hd = A_HEAD_DIM
    T = t_new
    R = A_GROUP * T
    S = 2 * W
    t_idx = lax.broadcasted_iota(jnp.int32, (R, S), 0) % T
    j_idx = lax.broadcasted_iota(jnp.int32, (R, S), 1)
    mask = ((j_idx < W) & (j_idx > t_idx)) | ((j_idx >= W) & (j_idx - W <= t_idx))
    nt = (((1,), (1,)), ((), ()))
    lane = lax.broadcasted_iota(jnp.int32, (A_KV_WIDTH, W), 1)
    pad = jnp.zeros((W - T, A_KV_WIDTH), F32)
    q_all = q_ref[...].astype(F32)
    kn_all = kn_ref[...].astype(F32)
    vn_all = vn_ref[...].astype(F32)
    jobs = []
    for s in range(n_seq):
        rows = slice(s * T, (s + 1) * T)
        cat = []
        for c_ref, new, out_ref in ((ck_ref, kn_all, ko_ref), (cv_ref, vn_all, vo_ref)):
            old = c_ref[s]
            fresh = jnp.concatenate([new[rows, :], pad], axis=0).T
            out_ref[s] = pltpu.roll(jnp.where(lane < T, fresh, old), W - T, 1)
            cat.append(jnp.concatenate([old, fresh], axis=1).astype(BF16))
        for kh in range(A_KV_HEADS):
            heads = [kh * A_GROUP + g for g in range(A_GROUP)]
            q4 = jnp.concatenate([q_all[rows, h * hd:(h + 1) * hd] for h in heads], axis=0) * (hd ** -0.5)
            kv = slice(kh * hd, (kh + 1) * hd)
            jobs.append(dict(heads=heads, values=cat[1][kv, :],
                             sc=jnp.dot(q4.astype(BF16), cat[0][kv, :], preferred_element_type=F32)))
    yield

    for job in jobs:
        sink = jnp.concatenate([jnp.full((T, 1), sink_ref[h], F32) for h in job["heads"]], axis=0)
        sc = jnp.where(mask, job["sc"], -jnp.inf)
        mx = jnp.maximum(sink, jnp.max(sc, axis=1, keepdims=True))
        p = jnp.exp(sc - mx)
        den = jnp.exp(sink - mx) + jnp.sum(p, axis=1, keepdims=True)
        job["o4"] = lax.dot_general(p.astype(BF16), job["values"], nt, preferred_element_type=F32) / den
    yield

    att_rows = []
    for s in range(n_seq):
        outs = []
        for job in jobs[s * A_KV_HEADS:(s + 1) * A_KV_HEADS]:
            outs += [job["o4"][g * T:(g + 1) * T, :] for g in range(A_GROUP)]
        att_rows.append(jnp.concatenate(outs, axis=1))
    att = att_rows[0] if n_seq == 1 else jnp.concatenate(att_rows, axis=0)
    o_ref[...] = (att * z_ref[...].astype(F32)).astype(o_ref.dtype)


def _swa_sample_side(pa_qz, pa_kv, cache_k, cache_v, sinks, *, n_steps, n_batch, t_new, row0):
    n_seq = n_batch // n_steps
    rows = n_seq * t_new
    blk0 = row0 // rows
    cache_spec = pl.BlockSpec((n_seq, A_KV_WIDTH, WINDOW), lambda s: (s, 0, 0))

    def fn(step, in_refs, out_refs):
        return _swa_sample_block(*in_refs, *out_refs, n_seq=n_seq, t_new=t_new)

    return dict(fn=fn, args=[sinks, pa_qz, pa_qz, pa_kv, pa_kv, cache_k, cache_v],
                in_specs=[pl.BlockSpec(memory_space=pltpu.SMEM),
                          pl.BlockSpec((rows, A_WIDTH), lambda s: (blk0 + s, 0)),
                          pl.BlockSpec((rows, A_WIDTH), lambda s: (blk0 + s, 1)),
                          pl.BlockSpec((rows, A_KV_WIDTH), lambda s: (blk0 + s, 0)),
                          pl.BlockSpec((rows, A_KV_WIDTH), lambda s: (blk0 + s, 1)),
                          cache_spec, cache_spec],
                out_specs=[pl.BlockSpec((rows, A_WIDTH), lambda s: (s, 0)), cache_spec, cache_spec],
                out_shape=[jax.ShapeDtypeStruct((n_batch * t_new, A_WIDTH), BF16),
                           jax.ShapeDtypeStruct(cache_k.shape, F32),
                           jax.ShapeDtypeStruct(cache_v.shape, F32)])


def _merge_tile(mm_ref, ma_ref, wbm_ref, wba_ref, gm_ref, ga_ref, o_ref):
    bm = jnp.dot(mm_ref[...], wbm_ref[...], preferred_element_type=F32)
    yield
    ba = jnp.dot(ma_ref[...], wba_ref[...], preferred_element_type=F32)
    yield
    o_ref[...] = (gm_ref[...].astype(F32) * bm + ga_ref[...].astype(F32) * ba).astype(o_ref.dtype)


def _merge(mix_m, mix_a, w_bm, w_ba, gates, *, m_rows, tn, row0, side=None, name):
    tm = min(1024, m_rows)
    nj = D_MODEL // tn
    blk0 = row0 // tm

    def fn(step, in_refs, out_refs):
        return _merge_tile(*in_refs, *out_refs)

    main = dict(fn=fn, args=[mix_m, mix_a, w_bm, w_ba, gates, gates],
                in_specs=[pl.BlockSpec((tm, M_WIDTH), lambda s: (s // nj, 0)),
                          pl.BlockSpec((tm, A_WIDTH), lambda s: (s // nj, 0)),
                          pl.BlockSpec((M_WIDTH, tn), lambda s: (0, s % nj)),
                          pl.BlockSpec((A_WIDTH, tn), lambda s: (0, s % nj)),
                          pl.BlockSpec((tm, tn), lambda s: (blk0 + s // nj, s % nj)),
                          pl.BlockSpec((tm, tn), lambda s: (blk0 + s // nj, s % nj + nj))],
                out_specs=[pl.BlockSpec((tm, tn), lambda s: (s // nj, s % nj))],
                out_shape=[jax.ShapeDtypeStruct((m_rows, D_MODEL), BF16)])
    (out,), side_outs = _hosted_call((m_rows // tm) * nj, main, side, name)
    return out, side_outs


def _out_tile(j, mg_ref, w_ref, x_ref, g_ref, b_ref, o_ref, *, tn):
    nj = o_ref.shape[1] // tn
    o_ref[:, pl.ds(pl.multiple_of(j * tn, tn), tn)] = (
        DEEPNORM_ALPHA * x_ref[...] + jnp.dot(mg_ref[...], w_ref[...], preferred_element_type=F32))
    yield

    @pl.when(j == nj - 1)
    def _():
        chunk = 8 * SUBLANES

        def norm_rows(r, carry):
            rows = pl.ds(pl.multiple_of(r * chunk, chunk), chunk)
            y = o_ref[rows, :]
            yc = y - jnp.mean(y, axis=1, keepdims=True)
            var = jnp.mean(jnp.square(yc), axis=1, keepdims=True)
            o_ref[rows, :] = yc * lax.rsqrt(var + LN_EPS) * g_ref[...] + b_ref[...]
            return carry

        lax.fori_loop(0, o_ref.shape[0] // chunk, norm_rows, 0)


def _out_proj(merged, w_out, x2d, ln_g, ln_b, *, tm, tn, side=None, name):
    m_rows = merged.shape[0]
    nj = D_MODEL // tn

    def fn(step, in_refs, out_refs):
        return _out_tile(step % nj, *in_refs, *out_refs, tn=tn)

    main = dict(fn=fn, args=[merged, w_out, x2d, ln_g, ln_b],
                in_specs=[pl.BlockSpec((tm, D_MODEL), lambda s: (s // nj, 0)),
                          pl.BlockSpec((D_MODEL, tn), lambda s: (0, s % nj)),
                          pl.BlockSpec((tm, tn), lambda s: (s // nj, s % nj)),
                          pl.BlockSpec((1, D_MODEL), lambda s: (0, 0)),
                          pl.BlockSpec((1, D_MODEL), lambda s: (0, 0))],
                out_specs=[pl.BlockSpec((tm, D_MODEL), lambda s: (s // nj, 0))],
                out_shape=[jax.ShapeDtypeStruct((m_rows, D_MODEL), F32)])
    (out,), side_outs = _hosted_call((m_rows // tm) * nj, main, side, name)
    return out, side_outs


def _rope_tables(positions):
    half = A_HEAD_DIM // 2
    lane = jnp.arange(LANES)
    inv = ROPE_THETA ** (-(lane % half).astype(F32) / half)
    ang = positions.astype(F32)[:, None] * inv[None, :]
    sign = jnp.where((lane % A_HEAD_DIM) < half, -1.0, 1.0).astype(F32)
    return jnp.cos(ang), jnp.sin(ang) * sign[None, :]


def kernel(x_prompt, x_sample, state_C, state_n, state_m, cache_k, cache_v, w_in, b_if, norm_m_g,
           attn_sinks, w_bm, w_ba, w_out, ln_g, ln_b):
    bp, seq, _ = x_prompt.shape
    bs, t_new, _ = x_sample.shape
    w_buf = cache_k.shape[2]
    assert DEPTH == 1 and w_buf == WINDOW and seq % WINDOW == 0
    n_p, n_s = bp * seq, bs * t_new

    wt = jnp.swapaxes(w_in[0], 0, 1)
    w_attn = _repack_attn(wt)
    bif = jnp.pad(b_if[0], (0, LANES - 2 * M_HEADS)).reshape(1, LANES)
    ng = norm_m_g[0].reshape(1, M_WIDTH)
    sinks = attn_sinks[0]
    lng, lnb = ln_g[0].reshape(1, D_MODEL), ln_b[0].reshape(1, D_MODEL)

    xp2 = x_prompt.reshape(n_p, D_MODEL)
    xs2 = x_sample.reshape(n_s, D_MODEL)
    cos_p, sin_p = _rope_tables(jnp.arange(seq))
    cos_s, sin_s = _rope_tables(PAST_LEN + jnp.arange(t_new))
    cos = jnp.concatenate([jnp.tile(cos_p, (bp, 1)), jnp.tile(cos_s, (bs, 1))], axis=0)
    sin = jnp.concatenate([jnp.tile(sin_p, (bp, 1)), jnp.tile(sin_s, (bs, 1))], axis=0)
    tn = PROJ_TN
    n_m = (n_p + n_s) // PROJ_TM
    blk_gate = MLSTM_COLS // tn
    blk_if = ATTN_COLS // LANES
    xb, pa_kv = _project_first(xp2, xs2, w_attn, cos, sin, tm=PROJ_TM // 4, tn=tn, name="proj_kv_cast")
    attn_tm = PROJ_TM // 2
    qz_tiles = 2 * A_WIDTH // tn
    repack_side = _repack_rest_side(wt, n_steps=qz_tiles * ((n_p + n_s) // attn_tm))
    if_side = _gate_proj_side(xb, w_attn, blk_if, n_steps=qz_tiles * ((n_p + n_s) // attn_tm))
    pa_qz, (w_rest, g_if) = _project(xb, w_attn, 2 * A_KV_WIDTH // tn, qz_tiles, tn=tn, tm=attn_tm,
                                     out_dtype=BF16, mode="attn", cos=cos, sin=sin,
                                     side=_join_sides(repack_side, if_side), name="proj_qz_repack")
    tn_m = MLSTM_TN
    swa_side = _swa_prompt_side(pa_qz, pa_kv, sinks, n_steps=(MLSTM_COLS // tn_m) * n_m, n_batch=bp, seq=seq)
    cast_side = _cast_rows_side([w_bm[0], w_ba[0], w_out[0]], n_slabs=64)
    pm, (mix_a_p, wbm, wba, wo) = _project(xb, w_rest, 0, MLSTM_COLS // tn_m, tn=tn_m, out_dtype=BF16, mode="mlstm",
                                           side=_join_sides(swa_side, cast_side), name="proj_mlstm_swa_p")
    mlstm_side = _mlstm_prompt_side(pm, g_if, bif, ng, n_steps=(GATE_COLS // tn) * n_m, seq=seq, chunk=128)
    gates, (mix_m_p, p_c, p_n, p_m) = _project(xb, w_rest, blk_gate, GATE_COLS // tn, tn=tn, out_dtype=BF16,
                                               mode="sigmoid", side=mlstm_side, name="proj_gate_mlstm_p")

    init = (state_C[0], state_n[0], jnp.broadcast_to(state_m[0][:, :, None], (bs, M_HEADS, LANES)))
    merge_tn, out_tm, out_tn = 512, 512, 1024
    mlstm_s_side = _mlstm_sample_side(pm, g_if, bif, ng, init, n_steps=(n_p // 1024) * (D_MODEL // merge_tn),
                                      n_batch=bs, t_new=t_new, row0=n_p)
    def window_minor(c):
        return jnp.transpose(c, (0, 2, 3, 1)).reshape(bs, A_KV_WIDTH, w_buf)

    def window_major(c):
        return jnp.transpose(c.reshape(bs, A_KV_HEADS, A_HEAD_DIM, w_buf), (0, 3, 1, 2))[None]

    swa_s_side = _swa_sample_side(pa_qz, pa_kv, window_minor(cache_k[0]), window_minor(cache_v[0]), sinks,
                                  n_steps=(n_p // out_tm) * (D_MODEL // out_tn), n_batch=bs, t_new=t_new, row0=n_p)

    merged, (mix_m_s, s_c, s_n, s_m) = _merge(mix_m_p, mix_a_p, wbm, wba, gates, m_rows=n_p, tn=merge_tn, row0=0,
                                              side=mlstm_s_side, name="merge_p_mlstm_s")
    y_p, (mix_a_s, s_k, s_v) = _out_proj(merged, wo, xp2, lng, lnb, tm=out_tm, tn=out_tn, side=swa_s_side,
                                         name="out_p_swa_s")
    y_p = y_p.reshape(bp, seq, D_MODEL)
    kv_last = jnp.stack([pa_kv[(b + 1) * seq - w_buf:(b + 1) * seq] for b in range(bp)]).astype(F32)
    p_k = kv_last[..., :A_KV_WIDTH].reshape(1, bp, w_buf, A_KV_HEADS, A_HEAD_DIM)
    p_v = kv_last[..., A_KV_WIDTH:].reshape(1, bp, w_buf, A_KV_HEADS, A_HEAD_DIM)
    p_c, p_n, p_m = p_c[:bp], p_n[:bp], p_m[:bp]

    merged, _ = _merge(mix_m_s, mix_a_s, wbm, wba, gates, m_rows=n_s, tn=merge_tn, row0=n_p, name="merge_s")
    y_s, _ = _out_proj(merged, wo, xs2, lng, lnb, tm=out_tm, tn=out_tn, name="out_s")
    y_s = y_s.reshape(bs, t_new, D_MODEL)

    def st(a, like):
        return a[None].astype(like.dtype)

    return (y_p, y_s,
            st(p_c, state_C), st(p_n, state_n), st(p_m[:, :, 0], state_m), p_k, p_v,
            st(s_c, state_C), st(s_n, state_n), st(s_m[:, :, 0], state_m),
            window_major(s_k), window_major(s_v))
```

```python
import functools

import jax
import jax.numpy as jnp
from jax import lax
from jax.experimental import pallas as pl
from jax.experimental.pallas import tpu as pltpu

F32 = jnp.float32
BF16 = jnp.bfloat16

D_MODEL = 4096
DEPTH = 1
PAST_LEN = 8192
M_HEADS = 8
M_WIDTH = D_MODEL // 2
M_HEAD_DIM = M_WIDTH // M_HEADS
A_HEAD_DIM = 64
A_WIDTH = D_MODEL // 2
A_HEADS = A_WIDTH // A_HEAD_DIM
A_KV_HEADS = A_HEADS // 4
A_GROUP = A_HEADS // A_KV_HEADS
A_KV_WIDTH = A_KV_HEADS * A_HEAD_DIM
WINDOW = 128
ROPE_THETA = 10000.0
LN_EPS = 1e-5
DEEPNORM_ALPHA = (2.0 * DEPTH) ** 0.25

LANES = 128
SUBLANES = 8
VMEM_LIMIT = 60 * 1024 * 1024

_OFF_IF = 5 * M_WIDTH
_OFF_QA = _OFF_IF + 2 * M_HEADS

PROJ_TM = 1024
PROJ_TN = 1024
MLSTM_TN = 1280
REPACK_TN = 512
REPACK_ROWS = 2048
MLSTM_COLS = 5 * M_WIDTH
ATTN_COLS = 2 * A_WIDTH + 2 * A_KV_WIDTH
GATE_COLS = 2 * D_MODEL
MAIN_COLS = MLSTM_COLS + ATTN_COLS + GATE_COLS


def _params(sem):
    return pltpu.CompilerParams(dimension_semantics=sem, vmem_limit_bytes=VMEM_LIMIT)


def _repack_src_block(t):
    n_main = MAIN_COLS // REPACK_TN
    za0 = (MLSTM_COLS + A_WIDTH) // REPACK_TN
    ka0 = za0 + A_WIDTH // REPACK_TN
    g0 = ka0 + 2 * A_KV_WIDTH // REPACK_TN
    kv_blocks = 2 * A_KV_WIDTH // REPACK_TN
    za_blocks = A_WIDTH // REPACK_TN
    moved = t + jnp.where((t >= za0) & (t < ka0), kv_blocks, 0) - jnp.where((t >= ka0) & (t < g0), za_blocks, 0)
    return jnp.where(t == n_main, _OFF_IF // REPACK_TN, moved)


def _repack_block(t, a_ref, b_ref, o_ref):
    n_aligned = _OFF_IF // REPACK_TN
    n_main = MAIN_COLS // REPACK_TN
    sh = _OFF_QA - _OFF_IF
    a = a_ref[...]
    moved = jnp.concatenate([a[sh:, :], b_ref[...]], axis=0)
    row = lax.broadcasted_iota(jnp.int32, a.shape, 0)
    gate_rows = jnp.where(row < sh, a, 0.0)
    src = jnp.where(t < n_aligned, a, jnp.where(t < n_main, moved, gate_rows))
    o_ref[...] = src.T.astype(o_ref.dtype)


_ATTN_TILE0 = MLSTM_COLS // REPACK_TN
_ATTN_TILES = ATTN_COLS // REPACK_TN
_FIRST_TILES = _ATTN_TILES + 1
_REST_TILES = MAIN_COLS // REPACK_TN - _ATTN_TILES
_ROW_HALVES = D_MODEL // REPACK_ROWS
_KV_TILES = 2 * A_KV_WIDTH // REPACK_TN


def _repack_specs(tile_of, half_of, out_tile_of):
    per = REPACK_TN // (_OFF_QA - _OFF_IF)
    return dict(
        in_specs=[pl.BlockSpec((REPACK_TN, REPACK_ROWS), lambda *g: (_repack_src_block(tile_of(*g)), half_of(*g))),
                  pl.BlockSpec((_OFF_QA - _OFF_IF, REPACK_ROWS),
                               lambda *g: ((_repack_src_block(tile_of(*g)) + 1) * per, half_of(*g)))],
        out_specs=[pl.BlockSpec((REPACK_ROWS, REPACK_TN), lambda *g: (half_of(*g), out_tile_of(*g)))])


def _repack_attn(wt):
    def tile_of(u, r=None):
        kv0 = _ATTN_TILE0 + _ATTN_TILES - _KV_TILES
        return jnp.where(u < _KV_TILES, kv0 + u,
                         jnp.where(u < _ATTN_TILES, _ATTN_TILE0 + u - _KV_TILES, MAIN_COLS // REPACK_TN))

    specs = _repack_specs(tile_of, lambda u, r: r, lambda u, r: u)

    def body(a_ref, b_ref, o_ref):
        _repack_block(tile_of(pl.program_id(0)), a_ref, b_ref, o_ref)

    return pl.pallas_call(
        body,
        grid=(_FIRST_TILES, _ROW_HALVES),
        in_specs=specs["in_specs"],
        out_specs=specs["out_specs"][0],
        out_shape=jax.ShapeDtypeStruct((D_MODEL, _FIRST_TILES * REPACK_TN), BF16),
        compiler_params=_params(("parallel", "parallel")),
        name="repack_attn",
    )(wt, wt)


def _repack_rest_side(wt, *, n_steps):
    n_blocks = _REST_TILES * _ROW_HALVES
    assert n_steps >= n_blocks

    def local(s):
        return jnp.minimum(s, n_blocks - 1) // _ROW_HALVES

    def tile_of(s):
        return local(s) + jnp.where(local(s) >= _ATTN_TILE0, _ATTN_TILES, 0)

    def half_of(s):
        return jnp.minimum(s, n_blocks - 1) % _ROW_HALVES

    def fn(step, in_refs, out_refs):
        _repack_block(tile_of(step), *in_refs, *out_refs)
        yield

    return dict(fn=fn, args=[wt, wt], out_shape=[jax.ShapeDtypeStruct((D_MODEL, _REST_TILES * REPACK_TN), BF16)],
                **_repack_specs(tile_of, half_of, local))


def _hosted_body(*refs, n_main_in, n_side_in, n_main_out, main_fn, side_fn):
    main_in = refs[:n_main_in]
    side_in = refs[n_main_in:n_main_in + n_side_in]
    main_out = refs[n_main_in + n_side_in:n_main_in + n_side_in + n_main_out]
    side_out = refs[n_main_in + n_side_in + n_main_out:]
    step = pl.program_id(0)
    stages = [fn(step, i, o) for fn, i, o in ((side_fn, side_in, side_out), (main_fn, main_in, main_out)) if fn]
    while stages:
        alive = []
        for g in stages:
            try:
                next(g)
                alive.append(g)
            except StopIteration:
                pass
        stages = alive


def _join_sides(*sides):
    def fn(step, in_refs, out_refs):
        gens, i0, o0 = [], 0, 0
        for sd in sides:
            ni, no = len(sd["args"]), len(sd["out_shape"])
            gens.append(sd["fn"](step, in_refs[i0:i0 + ni], out_refs[o0:o0 + no]))
            i0, o0 = i0 + ni, o0 + no
        while gens:
            alive = []
            for g in gens:
                try:
                    next(g)
                    alive.append(g)
                except StopIteration:
                    pass
            gens = alive
            if gens:
                yield

    return dict(fn=fn, **{k: sum((sd[k] for sd in sides), []) for k in ("args", "in_specs", "out_specs", "out_shape")})


def _cast_rows_side(mats, *, n_slabs):
    def fn(step, in_refs, out_refs):
        for src, dst in zip(in_refs, out_refs):
            dst[...] = src[...].astype(dst.dtype)
        yield

    specs = [pl.BlockSpec((m.shape[0] // n_slabs, m.shape[1]), lambda s: (jnp.minimum(s, n_slabs - 1), 0))
             for m in mats]
    return dict(fn=fn, args=list(mats), in_specs=specs, out_specs=list(specs),
                out_shape=[jax.ShapeDtypeStruct(m.shape, BF16) for m in mats])


def _gate_proj_side(xb, w, col_block, *, n_steps):
    rows = xb.shape[0] // n_steps
    assert rows * n_steps == xb.shape[0] and rows % SUBLANES == 0

    def fn(step, in_refs, out_refs):
        out_refs[0][...] = jnp.dot(in_refs[0][...], in_refs[1][...], preferred_element_type=F32)
        yield

    return dict(fn=fn, args=[xb, w],
                in_specs=[pl.BlockSpec((rows, D_MODEL), lambda s: (s, 0)),
                          pl.BlockSpec((D_MODEL, LANES), lambda s: (0, col_block))],
                out_specs=[pl.BlockSpec((rows, LANES), lambda s: (s, 0))],
                out_shape=[jax.ShapeDtypeStruct((xb.shape[0], LANES), F32)])


def _hosted_call(n_steps, main, side, name):
    side = side or dict(fn=None, args=[], in_specs=[], out_specs=[], out_shape=[])
    outs = pl.pallas_call(
        functools.partial(_hosted_body, n_main_in=len(main["args"]), n_side_in=len(side["args"]),
                          n_main_out=len(main["out_shape"]), main_fn=main["fn"], side_fn=side["fn"]),
        grid=(n_steps,),
        in_specs=main["in_specs"] + side["in_specs"],
        out_specs=main["out_specs"] + side["out_specs"],
        out_shape=main["out_shape"] + side["out_shape"],
        compiler_params=_params(("arbitrary",)),
        name=name,
    )(*main["args"], *side["args"])
    n_main = len(main["out_shape"])
    return outs[:n_main], outs[n_main:]


def _rope_slab(slab, cos, sin, lo):
    partner = jnp.where(lo, pltpu.roll(slab, LANES - 32, 1), pltpu.roll(slab, 32, 1))
    return slab * cos + partner * sin


def _proj_tile(j, x_ref, w_ref, rest, mode):
    tm, tn = x_ref.shape[0], w_ref.shape[1]
    pw = next((c for c in (4 * LANES, 2 * LANES) if tn % c == 0), tn)
    n_parts = tn // pw
    o_ref = rest[-1]

    if mode == "plain":
        def epilogue(acc, col0):
            return acc
    elif mode == "sigmoid":
        def epilogue(acc, col0):
            return jax.nn.sigmoid(acc)
    elif mode == "mlstm":
        per = M_WIDTH // LANES

        def epilogue(acc, col0):
            sig = jax.nn.sigmoid(acc)
            outs = []
            for c in range(acc.shape[1] // LANES):
                slab = j * (tn // LANES) + col0 // LANES + c
                a, s = acc[:, c * LANES:(c + 1) * LANES], sig[:, c * LANES:(c + 1) * LANES]
                outs.append(jnp.where(slab < 3 * per, a, jnp.where(slab < 4 * per, s, a * s)))
            return jnp.concatenate(outs, axis=1)
    else:
        cos_ref, sin_ref, _ = rest
        n_q = A_WIDTH // tn
        n_plain = n_q + A_WIDTH // tn
        rope_w = A_KV_WIDTH
        lo = (lax.broadcasted_iota(jnp.int32, (tm, LANES), 1) % A_HEAD_DIM) < (A_HEAD_DIM // 2)

        def epilogue(acc, col0):
            outs = []
            for c in range(acc.shape[1] // LANES):
                slab = acc[:, c * LANES:(c + 1) * LANES]
                roped = _rope_slab(slab, cos_ref[...], sin_ref[...], lo)
                use_rope = (j < n_q) | (j >= n_plain) if col0 + c * LANES < rope_w else j < n_q
                outs.append(jnp.where(use_rope, roped, jnp.where(j < n_plain, slab * jax.nn.sigmoid(slab), slab)))
            return jnp.concatenate(outs, axis=1)

    for p in range(n_parts):
        acc = jnp.dot(x_ref[...], w_ref[:, p * pw:(p + 1) * pw], preferred_element_type=F32)
        o_ref[:, p * pw:(p + 1) * pw] = epilogue(acc, p * pw).astype(o_ref.dtype)
        yield


def _project(xb, w, col_block0, n_blocks, *, tn, out_dtype, mode, tm=PROJ_TM, cos=None, sin=None, side=None,
             name):
    m_rows = xb.shape[0]
    tm = min(tm, m_rows)
    n_m = m_rows // tm
    in_specs = [pl.BlockSpec((tm, D_MODEL), lambda s: (s % n_m, 0)),
                pl.BlockSpec((D_MODEL, tn), lambda s: (0, s // n_m + col_block0))]
    args = [xb, w]
    if mode == "attn":
        in_specs += [pl.BlockSpec((tm, LANES), lambda s: (s % n_m, 0))] * 2
        args += [cos, sin]

    def fn(step, in_refs, out_refs):
        return _proj_tile(step // n_m, in_refs[0], in_refs[1], tuple(in_refs[2:]) + tuple(out_refs), mode)

    main = dict(fn=fn, args=args, in_specs=in_specs,
                out_specs=[pl.BlockSpec((tm, tn), lambda s: (s % n_m, s // n_m))],
                out_shape=[jax.ShapeDtypeStruct((m_rows, n_blocks * tn), out_dtype)])
    (out,), side_outs = _hosted_call(n_blocks * n_m, main, side, name)
    return out, side_outs


def _project_first(xp2, xs2, w, cos, sin, *, tm, tn, name):
    n_p, n_s = xp2.shape[0] // tm, xs2.shape[0] // tm
    kv_tile = 2 * (A_WIDTH // tn)

    def body(xp_ref, xs_ref, w_ref, cos_ref, sin_ref, xb_ref, o_ref):
        i = pl.program_id(0)
        xb_ref[...] = jnp.where(i < n_p, xp_ref[...], xs_ref[...]).astype(xb_ref.dtype)
        for _ in _proj_tile(kv_tile, xb_ref, w_ref, (cos_ref, sin_ref, o_ref), "attn"):
            pass

    return pl.pallas_call(
        body,
        grid=(n_p + n_s,),
        in_specs=[pl.BlockSpec((tm, D_MODEL), lambda i: (jnp.minimum(i, n_p - 1), 0)),
                  pl.BlockSpec((tm, D_MODEL), lambda i: (jnp.maximum(i - n_p, 0), 0), pipeline_mode=pl.Buffered(1)),
                  pl.BlockSpec((D_MODEL, tn), lambda i: (0, 0), pipeline_mode=pl.Buffered(1)),
                  pl.BlockSpec((tm, LANES), lambda i: (i, 0)),
                  pl.BlockSpec((tm, LANES), lambda i: (i, 0))],
        out_specs=[pl.BlockSpec((tm, D_MODEL), lambda i: (i, 0)),
                   pl.BlockSpec((tm, tn), lambda i: (i, 0))],
        out_shape=[jax.ShapeDtypeStruct(((n_p + n_s) * tm, D_MODEL), BF16),
                   jax.ShapeDtypeStruct(((n_p + n_s) * tm, tn), BF16)],
        compiler_params=_params(("arbitrary",)),
        name=name,
    )(xp2, xs2, w, cos, sin)


def _log_sigmoid(x):
    return jnp.minimum(x, 0.0) - jnp.log(1.0 + jnp.exp(-jnp.abs(x)))


def _cumsum_rows(a):
    n = a.shape[0]
    row = lax.broadcasted_iota(jnp.int32, a.shape, 0)
    shift = 1
    while shift < n:
        a = a + jnp.where(row >= shift, pltpu.roll(a, shift, 0), 0.0)
        shift *= 2
    return a


def _mlstm_chunk(first_chunk, refs, *, chunk, n_seq, has_init):
    q_ref, k_ref, v_ref, o_ref, z_ref, g_ref, bif_ref, ng_ref = refs[:8]
    if has_init:
        c0_ref, n0_ref, m0_ref = refs[8:11]
        mix_ref, c_ref, n_ref, m_ref = refs[11:]
    else:
        mix_ref, c_ref, n_ref, m_ref = refs[8:]
        c0_ref, n0_ref, m0_ref = c_ref, n_ref, m_ref

        @pl.when(first_chunk)
        def _():
            c_ref[...] = jnp.zeros_like(c_ref)
            n_ref[...] = jnp.zeros_like(n_ref)
            m_ref[...] = jnp.zeros_like(m_ref)

    L = chunk
    dh = M_HEAD_DIM
    lane = lax.broadcasted_iota(jnp.int32, (L, LANES), 1)
    causal = (lax.broadcasted_iota(jnp.int32, (L, L), 1) <= lax.broadcasted_iota(jnp.int32, (L, L), 0))
    pad_rows = (-L) % LANES

    gate_cols, gate_rows = [], []
    for s in range(n_seq):
        g = g_ref[s * L:(s + 1) * L, :] + bif_ref[...]
        b = _cumsum_rows(jnp.where(lane >= M_HEADS, _log_sigmoid(g), 0.0))
        col = jnp.where(lane < M_HEADS, g, b)
        colp = jnp.concatenate([col, jnp.zeros((pad_rows, LANES), F32)], axis=0) if pad_rows else col
        gate_cols.append(col)
        gate_rows.append(colp.T)

    def per_seq(ref, cols):
        a = ref[:, cols]
        if n_seq == 1:
            return [a]
        a = a.astype(F32)
        return [a[s * L:(s + 1) * L] for s in range(n_seq)]

    jobs = []
    for h in range(M_HEADS):
        cols = slice(h * dh, (h + 1) * dh)
        qs, ks, vs = per_seq(q_ref, cols), per_seq(k_ref, cols), per_seq(v_ref, cols)
        for s in range(n_seq):
            q = qs[s]
            k = ks[s] * (dh ** -0.5)
            qb, kb, vb = q.astype(BF16), k.astype(BF16), vs[s].astype(BF16)
            i_c = gate_cols[s][:, h:h + 1]
            b_c = gate_cols[s][:, M_HEADS + h:M_HEADS + h + 1]
            i_r = gate_rows[s][h:h + 1, :L]
            b_r = gate_rows[s][M_HEADS + h:M_HEADS + h + 1, :L]
            m_prev = m0_ref[s, h:h + 1, 0:1]

            log_d = jnp.where(causal, (b_c - b_r) + i_r, -jnp.inf)
            a_c = b_c + m_prev
            m_t = jnp.maximum(a_c, jnp.max(log_d, axis=1, keepdims=True))
            m_new = m_t[L - 1:L, :]
            b_last = b_c[L - 1:L, :]
            w_c = jnp.exp((b_last - b_c) + i_c - m_new)
            kw = k.astype(F32) * w_c
            jobs.append(dict(
                h=h, s=s, q=q, vb=vb, log_d=log_d, a_c=a_c, m_t=m_t, m_new=m_new, kw=kw,
                decay=jnp.exp(b_last + m_prev - m_new),
                qk=lax.dot_general(qb, kb, (((1,), (1,)), ((), ())), preferred_element_type=F32),
                qc=jnp.dot(qb, c0_ref[s, h].astype(BF16), preferred_element_type=F32)))
    yield

    for job in jobs:
        job["sc"] = job["qk"] * jnp.exp(job["log_d"] - job["m_t"])
        job["sv"] = jnp.dot(job["sc"].astype(BF16), job["vb"], preferred_element_type=F32)
    yield

    for h in range(M_HEADS):
        cols = slice(h * dh, (h + 1) * dh)
        os_, zs = per_seq(o_ref, cols), per_seq(z_ref, cols)
        mix_parts = []
        for s in range(n_seq):
            job = jobs[h * n_seq + s]
            n_prev = n0_ref[s, h:h + 1, :]
            inter = jnp.exp(job["a_c"] - job["m_t"])
            num = job["sv"] + inter * job["qc"]
            qn = jnp.sum(job["q"].astype(F32) * n_prev, axis=1, keepdims=True)
            den = jnp.sum(job["sc"], axis=1, keepdims=True) + inter * qn
            hid = num / jnp.maximum(jnp.abs(den), jnp.exp(-job["m_t"]))

            c_ref[s, h] = job["decay"] * c0_ref[s, h] + lax.dot_general(
                job["kw"].astype(BF16), job["vb"], (((0,), (0,)), ((), ())), preferred_element_type=F32)
            n_ref[s, h:h + 1, :] = job["decay"] * n_prev + jnp.sum(job["kw"], axis=0, keepdims=True)
            m_ref[s, h:h + 1, :] = jnp.broadcast_to(job["m_new"], (1, LANES))

            hid = os_[s].astype(F32) * hid
            mu = jnp.mean(hid, axis=1, keepdims=True)
            var = jnp.mean(jnp.square(hid - mu), axis=1, keepdims=True)
            hid = (hid - mu) * lax.rsqrt(var + LN_EPS) * ng_ref[:, cols]
            mix_parts.append(hid * zs[s].astype(F32))
        mix = mix_parts[0] if n_seq == 1 else jnp.concatenate(mix_parts, axis=0)
        mix_ref[:, cols] = mix.astype(mix_ref.dtype)


def _mlstm_state_shapes(n_slots):
    return [jax.ShapeDtypeStruct((n_slots, M_HEADS, M_HEAD_DIM, M_HEAD_DIM), F32),
            jax.ShapeDtypeStruct((n_slots, M_HEADS, M_HEAD_DIM), F32),
            jax.ShapeDtypeStruct((n_slots, M_HEADS, LANES), F32)]


def _mlstm_prompt_side(pm, gates, bif, ng, *, n_steps, seq, chunk):
    n_chunks = seq // chunk
    n_slots = -(-n_steps // n_chunks)
    assert n_steps * chunk <= pm.shape[0]
    st_specs = [pl.BlockSpec((1, M_HEADS, M_HEAD_DIM, M_HEAD_DIM), lambda s: (s // n_chunks, 0, 0, 0)),
                pl.BlockSpec((1, M_HEADS, M_HEAD_DIM), lambda s: (s // n_chunks, 0, 0)),
                pl.BlockSpec((1, M_HEADS, LANES), lambda s: (s // n_chunks, 0, 0))]

    def fn(step, in_refs, out_refs):
        return _mlstm_chunk(step % n_chunks == 0, tuple(in_refs) + tuple(out_refs), chunk=chunk, n_seq=1,
                            has_init=False)

    def grp(g):
        return pl.BlockSpec((chunk, M_WIDTH), lambda s: (s, g))

    return dict(fn=fn, args=[pm, pm, pm, pm, pm, gates, bif, ng],
                in_specs=[grp(g) for g in range(5)] + [pl.BlockSpec((chunk, LANES), lambda s: (s, 0)),
                                                       pl.BlockSpec((1, LANES), lambda s: (0, 0)),
                                                       pl.BlockSpec((1, M_WIDTH), lambda s: (0, 0))],
                out_specs=[pl.BlockSpec((chunk, M_WIDTH), lambda s: (s, 0))] + st_specs,
                out_shape=[jax.ShapeDtypeStruct((n_steps * chunk, M_WIDTH), BF16)] + _mlstm_state_shapes(n_slots))


def _mlstm_sample_side(pm, gates, bif, ng, init, *, n_steps, n_batch, t_new, row0):
    n_seq = n_batch // n_steps
    rows = n_seq * t_new
    blk0 = row0 // rows
    in_specs = [pl.BlockSpec((rows, M_WIDTH), lambda s, g=g: (blk0 + s, g)) for g in range(5)]
    in_specs += [pl.BlockSpec((rows, LANES), lambda s: (blk0 + s, 0)),
                 pl.BlockSpec((1, LANES), lambda s: (0, 0)),
                 pl.BlockSpec((1, M_WIDTH), lambda s: (0, 0))]
    st_specs = [pl.BlockSpec((n_seq, M_HEADS, M_HEAD_DIM, M_HEAD_DIM), lambda s: (s, 0, 0, 0)),
                pl.BlockSpec((n_seq, M_HEADS, M_HEAD_DIM), lambda s: (s, 0, 0)),
                pl.BlockSpec((n_seq, M_HEADS, LANES), lambda s: (s, 0, 0))]

    def fn(step, in_refs, out_refs):
        return _mlstm_chunk(None, tuple(in_refs) + tuple(out_refs), chunk=t_new, n_seq=n_seq, has_init=True)

    return dict(fn=fn, args=[pm, pm, pm, pm, pm, gates, bif, ng, *init],
                in_specs=in_specs + st_specs,
                out_specs=[pl.BlockSpec((rows, M_WIDTH), lambda s: (s, 0))] + st_specs,
                out_shape=[jax.ShapeDtypeStruct((n_batch * t_new, M_WIDTH), BF16)] + _mlstm_state_shapes(n_batch))


def _swa_prompt_block(has_prev, sink_ref, q_ref, z_ref, kc_ref, kp_ref, vc_ref, vp_ref, o_ref):
    W = WINDOW
    hd = A_HEAD_DIM
    row = lax.broadcasted_iota(jnp.int32, (W, LANES), 0)
    col = lax.broadcasted_iota(jnp.int32, (W, LANES), 1)
    causal = col <= row
    prev_ok = jnp.logical_and(jnp.logical_not(causal), has_prev)
    lo = col < hd
    nt = (((1,), (1,)), ((), ()))
    slabs_per_kv = A_GROUP * hd // LANES

    def swap_halves(a):
        return pltpu.roll(a.astype(F32), hd, 1).astype(BF16)

    jobs = []
    for pair in range(A_KV_WIDTH // LANES):
        ks = slice(pair * LANES, (pair + 1) * LANES)
        kc, kp, vc, vp = kc_ref[:, ks], kp_ref[:, ks], vc_ref[:, ks], vp_ref[:, ks]
        kc_s, kp_s, vc_s, vp_s = swap_halves(kc), swap_halves(kp), swap_halves(vc), swap_halves(vp)
        zero = jnp.zeros_like(kc)
        for sub in range(2):
            at_lo = (kc, kp) if sub == 0 else (kc_s, kp_s)
            at_hi = (kc_s, kp_s) if sub == 0 else (kc, kp)
            keys = jnp.concatenate([jnp.where(lo, at_lo[0], zero), jnp.where(lo, at_lo[1], zero),
                                    jnp.where(lo, zero, at_hi[0]), jnp.where(lo, zero, at_hi[1])], axis=0)
            v_lo = jnp.concatenate([vc, vp] if sub == 0 else [vc_s, vp_s], axis=0)
            v_hi = jnp.concatenate([vc_s, vp_s] if sub == 0 else [vc, vp], axis=0)
            kvh = 2 * pair + sub
            slabs = [kvh * slabs_per_kv + half for half in range(slabs_per_kv)]
            qs = jnp.concatenate([q_ref[:, sl * LANES:(sl + 1) * LANES] for sl in slabs], axis=0) * (hd ** -0.5)
            jobs.append(dict(slabs=slabs, values=(v_lo, v_hi),
                             sc_all=lax.dot_general(qs, keys, nt, preferred_element_type=F32)))
    yield

    causal_n = jnp.concatenate([causal] * slabs_per_kv, axis=0)
    prev_ok_n = jnp.concatenate([prev_ok] * slabs_per_kv, axis=0)
    for job in jobs:
        outs = []
        for par in range(2):
            s_c = job["sc_all"][:, (2 * par) * W:(2 * par + 1) * W]
            s_p = job["sc_all"][:, (2 * par + 1) * W:(2 * par + 2) * W]
            sc = jnp.where(causal_n, s_c, jnp.where(prev_ok_n, s_p, -jnp.inf))
            sink = jnp.concatenate([jnp.full((W, 1), sink_ref[2 * sl + par], F32) for sl in job["slabs"]], axis=0)
            mx = jnp.maximum(jnp.max(sc, axis=1, keepdims=True), sink)
            p = jnp.exp(sc - mx)
            den = jnp.sum(p, axis=1, keepdims=True) + jnp.exp(sink - mx)
            pb = jnp.concatenate([jnp.where(causal_n, p, 0.0), jnp.where(causal_n, 0.0, p)], axis=1)
            outs.append((jnp.dot(pb.astype(BF16), job["values"][par], preferred_element_type=F32), den))
        job["outs"] = outs
    yield

    for job in jobs:
        (pv0, den0), (pv1, den1) = job["outs"]
        for n, sl in enumerate(job["slabs"]):
            rows, cs = slice(n * W, (n + 1) * W), slice(sl * LANES, (sl + 1) * LANES)
            att = jnp.where(lo, pv0[rows] / den0[rows], pv1[rows] / den1[rows])
            o_ref[:, cs] = (att * z_ref[:, cs].astype(F32)).astype(o_ref.dtype)


def _swa_prompt_side(pa_qz, pa_kv, sinks, *, n_steps, n_batch, seq):
    nb = seq // WINDOW

    def cur(col):
        return lambda s: (jnp.minimum(s // nb, n_batch - 1) * nb + s % nb, col)

    def prev(col):
        return lambda s: (jnp.minimum(s // nb, n_batch - 1) * nb + jnp.maximum(s % nb - 1, 0), col)

    def fn(step, in_refs, out_refs):
        return _swa_prompt_block(step % nb > 0, *in_refs, *out_refs)

    return dict(fn=fn, args=[sinks, pa_qz, pa_qz, pa_kv, pa_kv, pa_kv, pa_kv],
                in_specs=[pl.BlockSpec(memory_space=pltpu.SMEM),
                          pl.BlockSpec((WINDOW, A_WIDTH), cur(0)),
                          pl.BlockSpec((WINDOW, A_WIDTH), cur(1)),
                          pl.BlockSpec((WINDOW, A_KV_WIDTH), cur(0)),
                          pl.BlockSpec((WINDOW, A_KV_WIDTH), prev(0)),
                          pl.BlockSpec((WINDOW, A_KV_WIDTH), cur(1)),
                          pl.BlockSpec((WINDOW, A_KV_WIDTH), prev(1))],
                out_specs=[pl.BlockSpec((WINDOW, A_WIDTH), lambda s: (s, 0))],
                out_shape=[jax.ShapeDtypeStruct((n_steps * WINDOW, A_WIDTH), BF16)])


def _swa_sample_block(sink_ref, q_ref, z_ref, kn_ref, vn_ref, ck_ref, cv_ref, o_ref, ko_ref, vo_ref, *, n_seq, t_new):
    W = WINDOW
    hd = A_HEAD_DIM
    T = t_new
    R = A_GROUP * T
    S = 2 * W
    t_idx = lax.broadcasted_iota(jnp.int32, (R, S), 0) % T
    j_idx = lax.broadcasted_iota(jnp.int32, (R, S), 1)
    mask = ((j_idx < W) & (j_idx > t_idx)) | ((j_idx >= W) & (j_idx - W <= t_idx))
    nt = (((1,), (1,)), ((), ()))
    lane = lax.broadcasted_iota(jnp.int32, (A_KV_WIDTH, W), 1)
    pad = jnp.zeros((W - T, A_KV_WIDTH), F32)
    q_all = q_ref[...].astype(F32)
    kn_all = kn_ref[...].astype(F32)
    vn_all = vn_ref[...].astype(F32)
    jobs = []
    for s in range(n_seq):
        rows = slice(s * T, (s + 1) * T)
        cat = []
        for c_ref, new, out_ref in ((ck_ref, kn_all, ko_ref), (cv_ref, vn_all, vo_ref)):
            old = c_ref[s]
            fresh = jnp.concatenate([new[rows, :], pad], axis=0).T
            out_ref[s] = pltpu.roll(jnp.where(lane < T, fresh, old), W - T, 1)
            cat.append(jnp.concatenate([old, fresh], axis=1).astype(BF16))
        for kh in range(A_KV_HEADS):
            heads = [kh * A_GROUP + g for g in range(A_GROUP)]
            q4 = jnp.concatenate([q_all[rows, h * hd:(h + 1) * hd] for h in heads], axis=0) * (hd ** -0.5)
            kv = slice(kh * hd, (kh + 1) * hd)
            jobs.append(dict(heads=heads, values=cat[1][kv, :],
                             sc=jnp.dot(q4.astype(BF16), cat[0][kv, :], preferred_element_type=F32)))
    yield

    for job in jobs:
        sink = jnp.concatenate([jnp.full((T, 1), sink_ref[h], F32) for h in job["heads"]], axis=0)
        sc = jnp.where(mask, job["sc"], -jnp.inf)
        mx = jnp.maximum(sink, jnp.max(sc, axis=1, keepdims=True))
        p = jnp.exp(sc - mx)
        den = jnp.exp(sink - mx) + jnp.sum(p, axis=1, keepdims=True)
        job["o4"] = lax.dot_general(p.astype(BF16), job["values"], nt, preferred_element_type=F32) / den
    yield

    att_rows = []
    for s in range(n_seq):
        outs = []
        for job in jobs[s * A_KV_HEADS:(s + 1) * A_KV_HEADS]:
            outs += [job["o4"][g * T:(g + 1) * T, :] for g in range(A_GROUP)]
        att_rows.append(jnp.concatenate(outs, axis=1))
    att = att_rows[0] if n_seq == 1 else jnp.concatenate(att_rows, axis=0)
    o_ref[...] = (att * z_ref[...].astype(F32)).astype(o_ref.dtype)


def _swa_sample_side(pa_qz, pa_kv, cache_k, cache_v, sinks, *, n_steps, n_batch, t_new, row0):
    n_seq = n_batch // n_steps
    rows = n_seq * t_new
    blk0 = row0 // rows
    cache_spec = pl.BlockSpec((n_seq, A_KV_WIDTH, WINDOW), lambda s: (s, 0, 0))

    def fn(step, in_refs, out_refs):
        return _swa_sample_block(*in_refs, *out_refs, n_seq=n_seq, t_new=t_new)

    return dict(fn=fn, args=[sinks, pa_qz, pa_qz, pa_kv, pa_kv, cache_k, cache_v],
                in_specs=[pl.BlockSpec(memory_space=pltpu.SMEM),
                          pl.BlockSpec((rows, A_WIDTH), lambda s: (blk0 + s, 0)),
                          pl.BlockSpec((rows, A_WIDTH), lambda s: (blk0 + s, 1)),
                          pl.BlockSpec((rows, A_KV_WIDTH), lambda s: (blk0 + s, 0)),
                          pl.BlockSpec((rows, A_KV_WIDTH), lambda s: (blk0 + s, 1)),
                          cache_spec, cache_spec],
                out_specs=[pl.BlockSpec((rows, A_WIDTH), lambda s: (s, 0)), cache_spec, cache_spec],
                out_shape=[jax.ShapeDtypeStruct((n_batch * t_new, A_WIDTH), BF16),
                           jax.ShapeDtypeStruct(cache_k.shape, F32),
                           jax.ShapeDtypeStruct(cache_v.shape, F32)])


def _merge_tile(mm_ref, ma_ref, wbm_ref, wba_ref, gm_ref, ga_ref, o_ref):
    bm = jnp.dot(mm_ref[...], wbm_ref[...], preferred_element_type=F32)
    yield
    ba = jnp.dot(ma_ref[...], wba_ref[...], preferred_element_type=F32)
    yield
    o_ref[...] = (gm_ref[...].astype(F32) * bm + ga_ref[...].astype(F32) * ba).astype(o_ref.dtype)


def _merge(mix_m, mix_a, w_bm, w_ba, gates, *, m_rows, tn, row0, side=None, name):
    tm = min(1024, m_rows)
    nj = D_MODEL // tn
    blk0 = row0 // tm

    def fn(step, in_refs, out_refs):
        return _merge_tile(*in_refs, *out_refs)

    main = dict(fn=fn, args=[mix_m, mix_a, w_bm, w_ba, gates, gates],
                in_specs=[pl.BlockSpec((tm, M_WIDTH), lambda s: (s // nj, 0)),
                          pl.BlockSpec((tm, A_WIDTH), lambda s: (s // nj, 0)),
                          pl.BlockSpec((M_WIDTH, tn), lambda s: (0, s % nj)),
                          pl.BlockSpec((A_WIDTH, tn), lambda s: (0, s % nj)),
                          pl.BlockSpec((tm, tn), lambda s: (blk0 + s // nj, s % nj)),
                          pl.BlockSpec((tm, tn), lambda s: (blk0 + s // nj, s % nj + nj))],
                out_specs=[pl.BlockSpec((tm, tn), lambda s: (s // nj, s % nj))],
                out_shape=[jax.ShapeDtypeStruct((m_rows, D_MODEL), BF16)])
    (out,), side_outs = _hosted_call((m_rows // tm) * nj, main, side, name)
    return out, side_outs


def _out_tile(j, mg_ref, w_ref, x_ref, g_ref, b_ref, o_ref, *, tn):
    nj = o_ref.shape[1] // tn
    o_ref[:, pl.ds(pl.multiple_of(j * tn, tn), tn)] = (
        DEEPNORM_ALPHA * x_ref[...] + jnp.dot(mg_ref[...], w_ref[...], preferred_element_type=F32))
    yield

    @pl.when(j == nj - 1)
    def _():
        chunk = 16 * SUBLANES

        def norm_rows(r, carry):
            rows = pl.ds(pl.multiple_of(r * chunk, chunk), chunk)
            y = o_ref[rows, :]
            yc = y - jnp.mean(y, axis=1, keepdims=True)
            var = jnp.mean(jnp.square(yc), axis=1, keepdims=True)
            o_ref[rows, :] = yc * lax.rsqrt(var + LN_EPS) * g_ref[...] + b_ref[...]
            return carry

        lax.fori_loop(0, o_ref.shape[0] // chunk, norm_rows, 0)


def _out_proj(merged, w_out, x2d, ln_g, ln_b, *, tm, tn, side=None, name):
    m_rows = merged.shape[0]
    nj = D_MODEL // tn

    def fn(step, in_refs, out_refs):
        return _out_tile(step % nj, *in_refs, *out_refs, tn=tn)

    main = dict(fn=fn, args=[merged, w_out, x2d, ln_g, ln_b],
                in_specs=[pl.BlockSpec((tm, D_MODEL), lambda s: (s // nj, 0)),
                          pl.BlockSpec((D_MODEL, tn), lambda s: (0, s % nj)),
                          pl.BlockSpec((tm, tn), lambda s: (s // nj, s % nj)),
                          pl.BlockSpec((1, D_MODEL), lambda s: (0, 0)),
                          pl.BlockSpec((1, D_MODEL), lambda s: (0, 0))],
                out_specs=[pl.BlockSpec((tm, D_MODEL), lambda s: (s // nj, 0))],
                out_shape=[jax.ShapeDtypeStruct((m_rows, D_MODEL), F32)])
    (out,), side_outs = _hosted_call((m_rows // tm) * nj, main, side, name)
    return out, side_outs


def _rope_tables(positions):
    half = A_HEAD_DIM // 2
    lane = jnp.arange(LANES)
    inv = ROPE_THETA ** (-(lane % half).astype(F32) / half)
    ang = positions.astype(F32)[:, None] * inv[None, :]
    sign = jnp.where((lane % A_HEAD_DIM) < half, -1.0, 1.0).astype(F32)
    return jnp.cos(ang), jnp.sin(ang) * sign[None, :]


def kernel(x_prompt, x_sample, state_C, state_n, state_m, cache_k, cache_v, w_in, b_if, norm_m_g,
           attn_sinks, w_bm, w_ba, w_out, ln_g, ln_b):
    bp, seq, _ = x_prompt.shape
    bs, t_new, _ = x_sample.shape
    w_buf = cache_k.shape[2]
    assert DEPTH == 1 and w_buf == WINDOW and seq % WINDOW == 0
    n_p, n_s = bp * seq, bs * t_new

    wt = jnp.swapaxes(w_in[0], 0, 1)
    w_attn = _repack_attn(wt)
    bif = jnp.pad(b_if[0], (0, LANES - 2 * M_HEADS)).reshape(1, LANES)
    ng = norm_m_g[0].reshape(1, M_WIDTH)
    sinks = attn_sinks[0]
    lng, lnb = ln_g[0].reshape(1, D_MODEL), ln_b[0].reshape(1, D_MODEL)

    xp2 = x_prompt.reshape(n_p, D_MODEL)
    xs2 = x_sample.reshape(n_s, D_MODEL)
    cos_p, sin_p = _rope_tables(jnp.arange(seq))
    cos_s, sin_s = _rope_tables(PAST_LEN + jnp.arange(t_new))
    cos = jnp.concatenate([jnp.tile(cos_p, (bp, 1)), jnp.tile(cos_s, (bs, 1))], axis=0)
    sin = jnp.concatenate([jnp.tile(sin_p, (bp, 1)), jnp.tile(sin_s, (bs, 1))], axis=0)
    tn = PROJ_TN
    n_m = (n_p + n_s) // PROJ_TM
    blk_gate = MLSTM_COLS // tn
    blk_if = ATTN_COLS // LANES
    xb, pa_kv = _project_first(xp2, xs2, w_attn, cos, sin, tm=PROJ_TM // 2, tn=tn, name="proj_kv_cast")
    attn_tm = PROJ_TM // 2
    qz_tiles = 2 * A_WIDTH // tn
    repack_side = _repack_rest_side(wt, n_steps=qz_tiles * ((n_p + n_s) // attn_tm))
    if_side = _gate_proj_side(xb, w_attn, blk_if, n_steps=qz_tiles * ((n_p + n_s) // attn_tm))
    pa_qz, (w_rest, g_if) = _project(xb, w_attn, 2 * A_KV_WIDTH // tn, qz_tiles, tn=tn, tm=attn_tm,
                                     out_dtype=BF16, mode="attn", cos=cos, sin=sin,
                                     side=_join_sides(repack_side, if_side), name="proj_qz_repack")
    tn_m = MLSTM_TN
    swa_side = _swa_prompt_side(pa_qz, pa_kv, sinks, n_steps=(MLSTM_COLS // tn_m) * n_m, n_batch=bp, seq=seq)
    cast_side = _cast_rows_side([w_bm[0], w_ba[0], w_out[0]], n_slabs=64)
    pm, (mix_a_p, wbm, wba, wo) = _project(xb, w_rest, 0, MLSTM_COLS // tn_m, tn=tn_m, out_dtype=BF16, mode="mlstm",
                                           side=_join_sides(swa_side, cast_side), name="proj_mlstm_swa_p")
    mlstm_side = _mlstm_prompt_side(pm, g_if, bif, ng, n_steps=(GATE_COLS // tn) * n_m, seq=seq, chunk=128)
    gates, (mix_m_p, p_c, p_n, p_m) = _project(xb, w_rest, blk_gate, GATE_COLS // tn, tn=tn, out_dtype=BF16,
                                               mode="sigmoid", side=mlstm_side, name="proj_gate_mlstm_p")

    init = (state_C[0], state_n[0], jnp.broadcast_to(state_m[0][:, :, None], (bs, M_HEADS, LANES)))
    merge_tn, out_tm, out_tn = 512, 512, 1024
    mlstm_s_side = _mlstm_sample_side(pm, g_if, bif, ng, init, n_steps=(n_p // 1024) * (D_MODEL // merge_tn),
                                      n_batch=bs, t_new=t_new, row0=n_p)
    def window_minor(c):
        return jnp.transpose(c, (0, 2, 3, 1)).reshape(bs, A_KV_WIDTH, w_buf)

    def window_major(c):
        return jnp.transpose(c.reshape(bs, A_KV_HEADS, A_HEAD_DIM, w_buf), (0, 3, 1, 2))[None]

    swa_s_side = _swa_sample_side(pa_qz, pa_kv, window_minor(cache_k[0]), window_minor(cache_v[0]), sinks,
                                  n_steps=(n_p // out_tm) * (D_MODEL // out_tn), n_batch=bs, t_new=t_new, row0=n_p)

    merged, (mix_m_s, s_c, s_n, s_m) = _merge(mix_m_p, mix_a_p, wbm, wba, gates, m_rows=n_p, tn=merge_tn, row0=0,
                                              side=mlstm_s_side, name="merge_p_mlstm_s")
    y_p, (mix_a_s, s_k, s_v) = _out_proj(merged, wo, xp2, lng, lnb, tm=out_tm, tn=out_tn, side=swa_s_side,
                                         name="out_p_swa_s")
    y_p = y_p.reshape(bp, seq, D_MODEL)
    kv_last = jnp.stack([pa_kv[(b + 1) * seq - w_buf:(b + 1) * seq] for b in range(bp)]).astype(F32)
    p_k = kv_last[..., :A_KV_WIDTH].reshape(1, bp, w_buf, A_KV_HEADS, A_HEAD_DIM)
    p_v = kv_last[..., A_KV_WIDTH:].reshape(1, bp, w_buf, A_KV_HEADS, A_HEAD_DIM)
    p_c, p_n, p_m = p_c[:bp], p_n[:bp], p_m[:bp]

    merged, _ = _merge(mix_m_s, mix_a_s, wbm, wba, gates, m_rows=n_s, tn=merge_tn, row0=n_p, name="merge_s")
    y_s, _ = _out_proj(merged, wo, xs2, lng, lnb, tm=out_tm, tn=out_tn, name="out_s")
    y_s = y_s.reshape(bs, t_new, D_MODEL)

    def st(a, like):
        return a[None].astype(like.dtype)

    return (y_p, y_s,
            st(p_c, state_C), st(p_n, state_n), st(p_m[:, :, 0], state_m), p_k, p_v,
            st(s_c, state_C), st(s_n, state_n), st(s_m[:, :, 0], state_m),
            window_major(s_k), window_major(s_v))
```

```python
import functools

import jax
import jax.numpy as jnp
from jax import lax
from jax.experimental import pallas as pl
from jax.experimental.pallas import tpu as pltpu

F32 = jnp.float32
BF16 = jnp.bfloat16

D_MODEL = 4096
DEPTH = 1
PAST_LEN = 8192
M_HEADS = 8
M_WIDTH = D_MODEL // 2
M_HEAD_DIM = M_WIDTH // M_HEADS
A_HEAD_DIM = 64
A_WIDTH = D_MODEL // 2
A_HEADS = A_WIDTH // A_HEAD_DIM
A_KV_HEADS = A_HEADS // 4
A_GROUP = A_HEADS // A_KV_HEADS
A_KV_WIDTH = A_KV_HEADS * A_HEAD_DIM
WINDOW = 128
ROPE_THETA = 10000.0
LN_EPS = 1e-5
DEEPNORM_ALPHA = (2.0 * DEPTH) ** 0.25

LANES = 128
SUBLANES = 8
VMEM_LIMIT = 60 * 1024 * 1024

_OFF_IF = 5 * M_WIDTH
_OFF_QA = _OFF_IF + 2 * M_HEADS

PROJ_TM = 1024
PROJ_TN = 1024
MLSTM_TN = 1280
REPACK_TN = 512
REPACK_ROWS = 2048
MLSTM_COLS = 5 * M_WIDTH
ATTN_COLS = 2 * A_WIDTH + 2 * A_KV_WIDTH
GATE_COLS = 2 * D_MODEL
MAIN_COLS = MLSTM_COLS + ATTN_COLS + GATE_COLS


def _params(sem):
    return pltpu.CompilerParams(dimension_semantics=sem, vmem_limit_bytes=VMEM_LIMIT)


def _repack_src_block(t):
    n_main = MAIN_COLS // REPACK_TN
    za0 = (MLSTM_COLS + A_WIDTH) // REPACK_TN
    ka0 = za0 + A_WIDTH // REPACK_TN
    g0 = ka0 + 2 * A_KV_WIDTH // REPACK_TN
    kv_blocks = 2 * A_KV_WIDTH // REPACK_TN
    za_blocks = A_WIDTH // REPACK_TN
    moved = t + jnp.where((t >= za0) & (t < ka0), kv_blocks, 0) - jnp.where((t >= ka0) & (t < g0), za_blocks, 0)
    return jnp.where(t == n_main, _OFF_IF // REPACK_TN, moved)


def _repack_block(t, a_ref, b_ref, o_ref):
    n_aligned = _OFF_IF // REPACK_TN
    n_main = MAIN_COLS // REPACK_TN
    sh = _OFF_QA - _OFF_IF
    a = a_ref[...]
    moved = jnp.concatenate([a[sh:, :], b_ref[...]], axis=0)
    row = lax.broadcasted_iota(jnp.int32, a.shape, 0)
    gate_rows = jnp.where(row < sh, a, 0.0)
    src = jnp.where(t < n_aligned, a, jnp.where(t < n_main, moved, gate_rows))
    o_ref[...] = src.T.astype(o_ref.dtype)


_ATTN_TILE0 = MLSTM_COLS // REPACK_TN
_ATTN_TILES = ATTN_COLS // REPACK_TN
_FIRST_TILES = _ATTN_TILES + 1
_REST_TILES = MAIN_COLS // REPACK_TN - _ATTN_TILES
_ROW_HALVES = D_MODEL // REPACK_ROWS
_KV_TILES = 2 * A_KV_WIDTH // REPACK_TN


def _repack_specs(tile_of, half_of, out_tile_of):
    per = REPACK_TN // (_OFF_QA - _OFF_IF)
    return dict(
        in_specs=[pl.BlockSpec((REPACK_TN, REPACK_ROWS), lambda *g: (_repack_src_block(tile_of(*g)), half_of(*g))),
                  pl.BlockSpec((_OFF_QA - _OFF_IF, REPACK_ROWS),
                               lambda *g: ((_repack_src_block(tile_of(*g)) + 1) * per, half_of(*g)))],
        out_specs=[pl.BlockSpec((REPACK_ROWS, REPACK_TN), lambda *g: (half_of(*g), out_tile_of(*g)))])


def _repack_attn(wt):
    def tile_of(u, r=None):
        kv0 = _ATTN_TILE0 + _ATTN_TILES - _KV_TILES
        return jnp.where(u < _KV_TILES, kv0 + u,
                         jnp.where(u < _ATTN_TILES, _ATTN_TILE0 + u - _KV_TILES, MAIN_COLS // REPACK_TN))

    specs = _repack_specs(tile_of, lambda u, r: r, lambda u, r: u)

    def body(a_ref, b_ref, o_ref):
        _repack_block(tile_of(pl.program_id(0)), a_ref, b_ref, o_ref)

    return pl.pallas_call(
        body,
        grid=(_FIRST_TILES, _ROW_HALVES),
        in_specs=specs["in_specs"],
        out_specs=specs["out_specs"][0],
        out_shape=jax.ShapeDtypeStruct((D_MODEL, _FIRST_TILES * REPACK_TN), BF16),
        compiler_params=_params(("parallel", "parallel")),
        name="repack_attn",
    )(wt, wt)


def _repack_rest_side(wt, *, n_steps):
    n_blocks = _REST_TILES * _ROW_HALVES
    assert n_steps >= n_blocks

    def local(s):
        return jnp.minimum(s, n_blocks - 1) // _ROW_HALVES

    def tile_of(s):
        return local(s) + jnp.where(local(s) >= _ATTN_TILE0, _ATTN_TILES, 0)

    def half_of(s):
        return jnp.minimum(s, n_blocks - 1) % _ROW_HALVES

    def fn(step, in_refs, out_refs):
        _repack_block(tile_of(step), *in_refs, *out_refs)
        yield

    return dict(fn=fn, args=[wt, wt], out_shape=[jax.ShapeDtypeStruct((D_MODEL, _REST_TILES * REPACK_TN), BF16)],
                **_repack_specs(tile_of, half_of, local))


def _hosted_body(*refs, n_main_in, n_side_in, n_main_out, main_fn, side_fn):
    main_in = refs[:n_main_in]
    side_in = refs[n_main_in:n_main_in + n_side_in]
    main_out = refs[n_main_in + n_side_in:n_main_in + n_side_in + n_main_out]
    side_out = refs[n_main_in + n_side_in + n_main_out:]
    step = pl.program_id(0)
    stages = [fn(step, i, o) for fn, i, o in ((side_fn, side_in, side_out), (main_fn, main_in, main_out)) if fn]
    while stages:
        alive = []
        for g in stages:
            try:
                next(g)
                alive.append(g)
            except StopIteration:
                pass
        stages = alive


def _join_sides(*sides):
    def fn(step, in_refs, out_refs):
        gens, i0, o0 = [], 0, 0
        for sd in sides:
            ni, no = len(sd["args"]), len(sd["out_shape"])
            gens.append(sd["fn"](step, in_refs[i0:i0 + ni], out_refs[o0:o0 + no]))
            i0, o0 = i0 + ni, o0 + no
        while gens:
            alive = []
            for g in gens:
                try:
                    next(g)
                    alive.append(g)
                except StopIteration:
                    pass
            gens = alive
            if gens:
                yield

    return dict(fn=fn, **{k: sum((sd[k] for sd in sides), []) for k in ("args", "in_specs", "out_specs", "out_shape")})


def _cast_rows_side(mats, *, n_slabs):
    def fn(step, in_refs, out_refs):
        for src, dst in zip(in_refs, out_refs):
            dst[...] = src[...].astype(dst.dtype)
        yield

    specs = [pl.BlockSpec((m.shape[0] // n_slabs, m.shape[1]), lambda s: (jnp.minimum(s, n_slabs - 1), 0))
             for m in mats]
    return dict(fn=fn, args=list(mats), in_specs=specs, out_specs=list(specs),
                out_shape=[jax.ShapeDtypeStruct(m.shape, BF16) for m in mats])


def _gate_proj_side(xb, w, col_block, *, n_steps):
    rows = xb.shape[0] // n_steps
    assert rows * n_steps == xb.shape[0] and rows % SUBLANES == 0

    def fn(step, in_refs, out_refs):
        out_refs[0][...] = jnp.dot(in_refs[0][...], in_refs[1][...], preferred_element_type=F32)
        yield

    return dict(fn=fn, args=[xb, w],
                in_specs=[pl.BlockSpec((rows, D_MODEL), lambda s: (s, 0)),
                          pl.BlockSpec((D_MODEL, LANES), lambda s: (0, col_block))],
                out_specs=[pl.BlockSpec((rows, LANES), lambda s: (s, 0))],
                out_shape=[jax.ShapeDtypeStruct((xb.shape[0], LANES), F32)])


def _hosted_call(n_steps, main, side, name):
    side = side or dict(fn=None, args=[], in_specs=[], out_specs=[], out_shape=[])
    outs = pl.pallas_call(
        functools.partial(_hosted_body, n_main_in=len(main["args"]), n_side_in=len(side["args"]),
                          n_main_out=len(main["out_shape"]), main_fn=main["fn"], side_fn=side["fn"]),
        grid=(n_steps,),
        in_specs=main["in_specs"] + side["in_specs"],
        out_specs=main["out_specs"] + side["out_specs"],
        out_shape=main["out_shape"] + side["out_shape"],
        compiler_params=_params(("arbitrary",)),
        name=name,
    )(*main["args"], *side["args"])
    n_main = len(main["out_shape"])
    return outs[:n_main], outs[n_main:]


def _rope_slab(slab, cos, sin, lo):
    partner = jnp.where(lo, pltpu.roll(slab, LANES - 32, 1), pltpu.roll(slab, 32, 1))
    return slab * cos + partner * sin


def _proj_tile(j, x_ref, w_ref, rest, mode):
    tm, tn = x_ref.shape[0], w_ref.shape[1]
    pw = next((c for c in (4 * LANES, 2 * LANES) if tn % c == 0), tn)
    n_parts = tn // pw
    o_ref = rest[-1]

    if mode == "plain":
        def epilogue(acc, col0):
            return acc
    elif mode == "sigmoid":
        def epilogue(acc, col0):
            return jax.nn.sigmoid(acc)
    elif mode == "mlstm":
        per = M_WIDTH // LANES

        def epilogue(acc, col0):
            sig = jax.nn.sigmoid(acc)
            outs = []
            for c in range(acc.shape[1] // LANES):
                slab = j * (tn // LANES) + col0 // LANES + c
                a, s = acc[:, c * LANES:(c + 1) * LANES], sig[:, c * LANES:(c + 1) * LANES]
                outs.append(jnp.where(slab < 3 * per, a, jnp.where(slab < 4 * per, s, a * s)))
            return jnp.concatenate(outs, axis=1)
    else:
        cos_ref, sin_ref, _ = rest
        n_q = A_WIDTH // tn
        n_plain = n_q + A_WIDTH // tn
        rope_w = A_KV_WIDTH
        lo = (lax.broadcasted_iota(jnp.int32, (tm, LANES), 1) % A_HEAD_DIM) < (A_HEAD_DIM // 2)

        def epilogue(acc, col0):
            outs = []
            for c in range(acc.shape[1] // LANES):
                slab = acc[:, c * LANES:(c + 1) * LANES]
                roped = _rope_slab(slab, cos_ref[...], sin_ref[...], lo)
                use_rope = (j < n_q) | (j >= n_plain) if col0 + c * LANES < rope_w else j < n_q
                outs.append(jnp.where(use_rope, roped, jnp.where(j < n_plain, slab * jax.nn.sigmoid(slab), slab)))
            return jnp.concatenate(outs, axis=1)

    for p in range(n_parts):
        acc = jnp.dot(x_ref[...], w_ref[:, p * pw:(p + 1) * pw], preferred_element_type=F32)
        o_ref[:, p * pw:(p + 1) * pw] = epilogue(acc, p * pw).astype(o_ref.dtype)
        yield


def _project(xb, w, col_block0, n_blocks, *, tn, out_dtype, mode, tm=PROJ_TM, cos=None, sin=None, side=None,
             name):
    m_rows = xb.shape[0]
    tm = min(tm, m_rows)
    n_m = m_rows // tm
    in_specs = [pl.BlockSpec((tm, D_MODEL), lambda s: (s % n_m, 0)),
                pl.BlockSpec((D_MODEL, tn), lambda s: (0, s // n_m + col_block0))]
    args = [xb, w]
    if mode == "attn":
        in_specs += [pl.BlockSpec((tm, LANES), lambda s: (s % n_m, 0))] * 2
        args += [cos, sin]

    def fn(step, in_refs, out_refs):
        return _proj_tile(step // n_m, in_refs[0], in_refs[1], tuple(in_refs[2:]) + tuple(out_refs), mode)

    main = dict(fn=fn, args=args, in_specs=in_specs,
                out_specs=[pl.BlockSpec((tm, tn), lambda s: (s % n_m, s // n_m))],
                out_shape=[jax.ShapeDtypeStruct((m_rows, n_blocks * tn), out_dtype)])
    (out,), side_outs = _hosted_call(n_blocks * n_m, main, side, name)
    return out, side_outs


def _project_first(xp2, xs2, w, cos, sin, *, tm, tn, name):
    n_p, n_s = xp2.shape[0] // tm, xs2.shape[0] // tm
    kv_tile = 2 * (A_WIDTH // tn)

    def body(xp_ref, xs_ref, w_ref, cos_ref, sin_ref, xb_ref, o_ref):
        i = pl.program_id(0)
        xb_ref[...] = jnp.where(i < n_p, xp_ref[...], xs_ref[...]).astype(xb_ref.dtype)
        for _ in _proj_tile(kv_tile, xb_ref, w_ref, (cos_ref, sin_ref, o_ref), "attn"):
            pass

    return pl.pallas_call(
        body,
        grid=(n_p + n_s,),
        in_specs=[pl.BlockSpec((tm, D_MODEL), lambda i: (jnp.minimum(i, n_p - 1), 0)),
                  pl.BlockSpec((tm, D_MODEL), lambda i: (jnp.maximum(i - n_p, 0), 0), pipeline_mode=pl.Buffered(1)),
                  pl.BlockSpec((D_MODEL, tn), lambda i: (0, 0), pipeline_mode=pl.Buffered(1)),
                  pl.BlockSpec((tm, LANES), lambda i: (i, 0)),
                  pl.BlockSpec((tm, LANES), lambda i: (i, 0))],
        out_specs=[pl.BlockSpec((tm, D_MODEL), lambda i: (i, 0)),
                   pl.BlockSpec((tm, tn), lambda i: (i, 0))],
        out_shape=[jax.ShapeDtypeStruct(((n_p + n_s) * tm, D_MODEL), BF16),
                   jax.ShapeDtypeStruct(((n_p + n_s) * tm, tn), BF16)],
        compiler_params=_params(("arbitrary",)),
        name=name,
    )(xp2, xs2, w, cos, sin)


def _log_sigmoid(x):
    return jnp.minimum(x, 0.0) - jnp.log(1.0 + jnp.exp(-jnp.abs(x)))


def _cumsum_rows(a):
    n = a.shape[0]
    row = lax.broadcasted_iota(jnp.int32, a.shape, 0)
    shift = 1
    while shift < n:
        a = a + jnp.where(row >= shift, pltpu.roll(a, shift, 0), 0.0)
        shift *= 2
    return a


def _mlstm_chunk(first_chunk, refs, *, chunk, n_seq, has_init):
    q_ref, k_ref, v_ref, o_ref, z_ref, g_ref, bif_ref, ng_ref = refs[:8]
    if has_init:
        c0_ref, n0_ref, m0_ref = refs[8:11]
        mix_ref, c_ref, n_ref, m_ref = refs[11:]
    else:
        mix_ref, c_ref, n_ref, m_ref = refs[8:]
        c0_ref, n0_ref, m0_ref = c_ref, n_ref, m_ref

        @pl.when(first_chunk)
        def _():
            c_ref[...] = jnp.zeros_like(c_ref)
            n_ref[...] = jnp.zeros_like(n_ref)
            m_ref[...] = jnp.zeros_like(m_ref)

    L = chunk
    dh = M_HEAD_DIM
    lane = lax.broadcasted_iota(jnp.int32, (L, LANES), 1)
    causal = (lax.broadcasted_iota(jnp.int32, (L, L), 1) <= lax.broadcasted_iota(jnp.int32, (L, L), 0))
    pad_rows = (-L) % LANES

    gate_cols, gate_rows = [], []
    for s in range(n_seq):
        g = g_ref[s * L:(s + 1) * L, :] + bif_ref[...]
        b = _cumsum_rows(jnp.where(lane >= M_HEADS, _log_sigmoid(g), 0.0))
        col = jnp.where(lane < M_HEADS, g, b)
        colp = jnp.concatenate([col, jnp.zeros((pad_rows, LANES), F32)], axis=0) if pad_rows else col
        gate_cols.append(col)
        gate_rows.append(colp.T)

    def per_seq(ref, cols):
        a = ref[:, cols]
        if n_seq == 1:
            return [a]
        a = a.astype(F32)
        return [a[s * L:(s + 1) * L] for s in range(n_seq)]

    jobs = []
    for h in range(M_HEADS):
        cols = slice(h * dh, (h + 1) * dh)
        qs, ks, vs = per_seq(q_ref, cols), per_seq(k_ref, cols), per_seq(v_ref, cols)
        for s in range(n_seq):
            q = qs[s]
            k = ks[s] * (dh ** -0.5)
            qb, kb, vb = q.astype(BF16), k.astype(BF16), vs[s].astype(BF16)
            i_c = gate_cols[s][:, h:h + 1]
            b_c = gate_cols[s][:, M_HEADS + h:M_HEADS + h + 1]
            i_r = gate_rows[s][h:h + 1, :L]
            b_r = gate_rows[s][M_HEADS + h:M_HEADS + h + 1, :L]
            m_prev = m0_ref[s, h:h + 1, 0:1]

            log_d = jnp.where(causal, (b_c - b_r) + i_r, -jnp.inf)
            a_c = b_c + m_prev
            m_t = jnp.maximum(a_c, jnp.max(log_d, axis=1, keepdims=True))
            m_new = m_t[L - 1:L, :]
            b_last = b_c[L - 1:L, :]
            w_c = jnp.exp((b_last - b_c) + i_c - m_new)
            kw = k.astype(F32) * w_c
            jobs.append(dict(
                h=h, s=s, q=q, vb=vb, log_d=log_d, a_c=a_c, m_t=m_t, m_new=m_new, kw=kw,
                decay=jnp.exp(b_last + m_prev - m_new),
                qk=lax.dot_general(qb, kb, (((1,), (1,)), ((), ())), preferred_element_type=F32),
                qc=jnp.dot(qb, c0_ref[s, h].astype(BF16), preferred_element_type=F32)))
    yield

    for job in jobs:
        job["sc"] = job["qk"] * jnp.exp(job["log_d"] - job["m_t"])
        job["sv"] = jnp.dot(job["sc"].astype(BF16), job["vb"], preferred_element_type=F32)
    yield

    for h in range(M_HEADS):
        cols = slice(h * dh, (h + 1) * dh)
        os_, zs = per_seq(o_ref, cols), per_seq(z_ref, cols)
        mix_parts = []
        for s in range(n_seq):
            job = jobs[h * n_seq + s]
            n_prev = n0_ref[s, h:h + 1, :]
            inter = jnp.exp(job["a_c"] - job["m_t"])
            num = job["sv"] + inter * job["qc"]
            qn = jnp.sum(job["q"].astype(F32) * n_prev, axis=1, keepdims=True)
            den = jnp.sum(job["sc"], axis=1, keepdims=True) + inter * qn
            hid = num / jnp.maximum(jnp.abs(den), jnp.exp(-job["m_t"]))

            c_ref[s, h] = job["decay"] * c0_ref[s, h] + lax.dot_general(
                job["kw"].astype(BF16), job["vb"], (((0,), (0,)), ((), ())), preferred_element_type=F32)
            n_ref[s, h:h + 1, :] = job["decay"] * n_prev + jnp.sum(job["kw"], axis=0, keepdims=True)
            m_ref[s, h:h + 1, :] = jnp.broadcast_to(job["m_new"], (1, LANES))

            hid = os_[s].astype(F32) * hid
            mu = jnp.mean(hid, axis=1, keepdims=True)
            var = jnp.mean(jnp.square(hid - mu), axis=1, keepdims=True)
            hid = (hid - mu) * lax.rsqrt(var + LN_EPS) * ng_ref[:, cols]
            mix_parts.append(hid * zs[s].astype(F32))
        mix = mix_parts[0] if n_seq == 1 else jnp.concatenate(mix_parts, axis=0)
        mix_ref[:, cols] = mix.astype(mix_ref.dtype)


def _mlstm_state_shapes(n_slots):
    return [jax.ShapeDtypeStruct((n_slots, M_HEADS, M_HEAD_DIM, M_HEAD_DIM), F32),
            jax.ShapeDtypeStruct((n_slots, M_HEADS, M_HEAD_DIM), F32),
            jax.ShapeDtypeStruct((n_slots, M_HEADS, LANES), F32)]


def _mlstm_prompt_side(pm, gates, bif, ng, *, n_steps, seq, chunk):
    n_chunks = seq // chunk
    n_slots = -(-n_steps // n_chunks)
    assert n_steps * chunk <= pm.shape[0]
    st_specs = [pl.BlockSpec((1, M_HEADS, M_HEAD_DIM, M_HEAD_DIM), lambda s: (s // n_chunks, 0, 0, 0)),
                pl.BlockSpec((1, M_HEADS, M_HEAD_DIM), lambda s: (s // n_chunks, 0, 0)),
                pl.BlockSpec((1, M_HEADS, LANES), lambda s: (s // n_chunks, 0, 0))]

    def fn(step, in_refs, out_refs):
        return _mlstm_chunk(step % n_chunks == 0, tuple(in_refs) + tuple(out_refs), chunk=chunk, n_seq=1,
                            has_init=False)

    def grp(g):
        return pl.BlockSpec((chunk, M_WIDTH), lambda s: (s, g))

    return dict(fn=fn, args=[pm, pm, pm, pm, pm, gates, bif, ng],
                in_specs=[grp(g) for g in range(5)] + [pl.BlockSpec((chunk, LANES), lambda s: (s, 0)),
                                                       pl.BlockSpec((1, LANES), lambda s: (0, 0)),
                                                       pl.BlockSpec((1, M_WIDTH), lambda s: (0, 0))],
                out_specs=[pl.BlockSpec((chunk, M_WIDTH), lambda s: (s, 0))] + st_specs,
                out_shape=[jax.ShapeDtypeStruct((n_steps * chunk, M_WIDTH), BF16)] + _mlstm_state_shapes(n_slots))


def _mlstm_sample_side(pm, gates, bif, ng, init, *, n_steps, n_batch, t_new, row0):
    n_seq = n_batch // n_steps
    rows = n_seq * t_new
    blk0 = row0 // rows
    in_specs = [pl.BlockSpec((rows, M_WIDTH), lambda s, g=g: (blk0 + s, g)) for g in range(5)]
    in_specs += [pl.BlockSpec((rows, LANES), lambda s: (blk0 + s, 0)),
                 pl.BlockSpec((1, LANES), lambda s: (0, 0)),
                 pl.BlockSpec((1, M_WIDTH), lambda s: (0, 0))]
    st_specs = [pl.BlockSpec((n_seq, M_HEADS, M_HEAD_DIM, M_HEAD_DIM), lambda s: (s, 0, 0, 0)),
                pl.BlockSpec((n_seq, M_HEADS, M_HEAD_DIM), lambda s: (s, 0, 0)),
                pl.BlockSpec((n_seq, M_HEADS, LANES), lambda s: (s, 0, 0))]

    def fn(step, in_refs, out_refs):
        return _mlstm_chunk(None, tuple(in_refs) + tuple(out_refs), chunk=t_new, n_seq=n_seq, has_init=True)

    return dict(fn=fn, args=[pm, pm, pm, pm, pm, gates, bif, ng, *init],
                in_specs=in_specs + st_specs,
                out_specs=[pl.BlockSpec((rows, M_WIDTH), lambda s: (s, 0))] + st_specs,
                out_shape=[jax.ShapeDtypeStruct((n_batch * t_new, M_WIDTH), BF16)] + _mlstm_state_shapes(n_batch))


def _swa_prompt_block(has_prev, sink_ref, q_ref, z_ref, kc_ref, kp_ref, vc_ref, vp_ref, o_ref):
    W = WINDOW
    hd = A_HEAD_DIM
    row = lax.broadcasted_iota(jnp.int32, (W, LANES), 0)
    col = lax.broadcasted_iota(jnp.int32, (W, LANES), 1)
    causal = col <= row
    prev_ok = jnp.logical_and(jnp.logical_not(causal), has_prev)
    lo = col < hd
    nt = (((1,), (1,)), ((), ()))
    slabs_per_kv = A_GROUP * hd // LANES

    def swap_halves(a):
        return pltpu.roll(a.astype(F32), hd, 1).astype(BF16)

    jobs = []
    for pair in range(A_KV_WIDTH // LANES):
        ks = slice(pair * LANES, (pair + 1) * LANES)
        kc, kp, vc, vp = kc_ref[:, ks], kp_ref[:, ks], vc_ref[:, ks], vp_ref[:, ks]
        kc_s, kp_s, vc_s, vp_s = swap_halves(kc), swap_halves(kp), swap_halves(vc), swap_halves(vp)
        zero = jnp.zeros_like(kc)
        for sub in range(2):
            at_lo = (kc, kp) if sub == 0 else (kc_s, kp_s)
            at_hi = (kc_s, kp_s) if sub == 0 else (kc, kp)
            keys = jnp.concatenate([jnp.where(lo, at_lo[0], zero), jnp.where(lo, at_lo[1], zero),
                                    jnp.where(lo, zero, at_hi[0]), jnp.where(lo, zero, at_hi[1])], axis=0)
            v_lo = jnp.concatenate([vc, vp] if sub == 0 else [vc_s, vp_s], axis=0)
            v_hi = jnp.concatenate([vc_s, vp_s] if sub == 0 else [vc, vp], axis=0)
            kvh = 2 * pair + sub
            slabs = [kvh * slabs_per_kv + half for half in range(slabs_per_kv)]
            qs = jnp.concatenate([q_ref[:, sl * LANES:(sl + 1) * LANES] for sl in slabs], axis=0) * (hd ** -0.5)
            jobs.append(dict(slabs=slabs, values=(v_lo, v_hi),
                             sc_all=lax.dot_general(qs, keys, nt, preferred_element_type=F32)))
    yield

    causal_n = jnp.concatenate([causal] * slabs_per_kv, axis=0)
    prev_ok_n = jnp.concatenate([prev_ok] * slabs_per_kv, axis=0)
    for job in jobs:
        outs = []
        for par in range(2):
            s_c = job["sc_all"][:, (2 * par) * W:(2 * par + 1) * W]
            s_p = job["sc_all"][:, (2 * par + 1) * W:(2 * par + 2) * W]
            sc = jnp.where(causal_n, s_c, jnp.where(prev_ok_n, s_p, -jnp.inf))
            sink = jnp.concatenate([jnp.full((W, 1), sink_ref[2 * sl + par], F32) for sl in job["slabs"]], axis=0)
            mx = jnp.maximum(jnp.max(sc, axis=1, keepdims=True), sink)
            p = jnp.exp(sc - mx)
            den = jnp.sum(p, axis=1, keepdims=True) + jnp.exp(sink - mx)
            pb = jnp.concatenate([jnp.where(causal_n, p, 0.0), jnp.where(causal_n, 0.0, p)], axis=1)
            outs.append((jnp.dot(pb.astype(BF16), job["values"][par], preferred_element_type=F32), den))
        job["outs"] = outs
    yield

    for job in jobs:
        (pv0, den0), (pv1, den1) = job["outs"]
        for n, sl in enumerate(job["slabs"]):
            rows, cs = slice(n * W, (n + 1) * W), slice(sl * LANES, (sl + 1) * LANES)
            att = jnp.where(lo, pv0[rows] / den0[rows], pv1[rows] / den1[rows])
            o_ref[:, cs] = (att * z_ref[:, cs].astype(F32)).astype(o_ref.dtype)


def _swa_prompt_side(pa_qz, pa_kv, sinks, *, n_steps, n_batch, seq):
    nb = seq // WINDOW

    def cur(col):
        return lambda s: (jnp.minimum(s // nb, n_batch - 1) * nb + s % nb, col)

    def prev(col):
        return lambda s: (jnp.minimum(s // nb, n_batch - 1) * nb + jnp.maximum(s % nb - 1, 0), col)

    def fn(step, in_refs, out_refs):
        return _swa_prompt_block(step % nb > 0, *in_refs, *out_refs)

    return dict(fn=fn, args=[sinks, pa_qz, pa_qz, pa_kv, pa_kv, pa_kv, pa_kv],
                in_specs=[pl.BlockSpec(memory_space=pltpu.SMEM),
                          pl.BlockSpec((WINDOW, A_WIDTH), cur(0)),
                          pl.BlockSpec((WINDOW, A_WIDTH), cur(1)),
                          pl.BlockSpec((WINDOW, A_KV_WIDTH), cur(0)),
                          pl.BlockSpec((WINDOW, A_KV_WIDTH), prev(0)),
                          pl.BlockSpec((WINDOW, A_KV_WIDTH), cur(1)),
                          pl.BlockSpec((WINDOW, A_KV_WIDTH), prev(1))],
                out_specs=[pl.BlockSpec((WINDOW, A_WIDTH), lambda s: (s, 0))],
                out_shape=[jax.ShapeDtypeStruct((n_steps * WINDOW, A_WIDTH), BF16)])


def _swa_sample_block(sink_ref, q_ref, z_ref, kn_ref, vn_ref, ck_ref, cv_ref, o_ref, ko_ref, vo_ref, *, n_seq, t_new):
    W = WINDOW
    hd = A_HEAD_DIM
    T = t_new
    R = A_GROUP * T
    S = 2 * W
    t_idx = lax.broadcasted_iota(jnp.int32, (R, S), 0) % T
    j_idx = lax.broadcasted_iota(jnp.int32, (R, S), 1)
    mask = ((j_idx < W) & (j_idx > t_idx)) | ((j_idx >= W) & (j_idx - W <= t_idx))
    nt = (((1,), (1,)), ((), ()))
    lane = lax.broadcasted_iota(jnp.int32, (A_KV_WIDTH, W), 1)
    pad = jnp.zeros((W - T, A_KV_WIDTH), F32)
    q_all = q_ref[...].astype(F32)
    kn_all = kn_ref[...].astype(F32)
    vn_all = vn_ref[...].astype(F32)
    jobs = []
    for s in range(n_seq):
        rows = slice(s * T, (s + 1) * T)
        cat = []
        for c_ref, new, out_ref in ((ck_ref, kn_all, ko_ref), (cv_ref, vn_all, vo_ref)):
            old = c_ref[s]
            fresh = jnp.concatenate([new[rows, :], pad], axis=0).T
            out_ref[s] = pltpu.roll(jnp.where(lane < T, fresh, old), W - T, 1)
            cat.append(jnp.concatenate([old, fresh], axis=1).astype(BF16))
        for kh in range(A_KV_HEADS):
            heads = [kh * A_GROUP + g for g in range(A_GROUP)]
            q4 = jnp.concatenate([q_all[rows, h * hd:(h + 1) * hd] for h in heads], axis=0) * (hd ** -0.5)
            kv = slice(kh * hd, (kh + 1) * hd)
            jobs.append(dict(heads=heads, values=cat[1][kv, :],
                             sc=jnp.dot(q4.astype(BF16), cat[0][kv, :], preferred_element_type=F32)))
    yield

    for job in jobs:
        sink = jnp.concatenate([jnp.full((T, 1), sink_ref[h], F32) for h in job["heads"]], axis=0)
        sc = jnp.where(mask, job["sc"], -jnp.inf)
        mx = jnp.maximum(sink, jnp.max(sc, axis=1, keepdims=True))
        p = jnp.exp(sc - mx)
        den = jnp.exp(sink - mx) + jnp.sum(p, axis=1, keepdims=True)
        job["o4"] = lax.dot_general(p.astype(BF16), job["values"], nt, preferred_element_type=F32) / den
    yield

    att_rows = []
    for s in range(n_seq):
        outs = []
        for job in jobs[s * A_KV_HEADS:(s + 1) * A_KV_HEADS]:
            outs += [job["o4"][g * T:(g + 1) * T, :] for g in range(A_GROUP)]
        att_rows.append(jnp.concatenate(outs, axis=1))
    att = att_rows[0] if n_seq == 1 else jnp.concatenate(att_rows, axis=0)
    o_ref[...] = (att * z_ref[...].astype(F32)).astype(o_ref.dtype)


def _swa_sample_side(pa_qz, pa_kv, cache_k, cache_v, sinks, *, n_steps, n_batch, t_new, row0):
    n_seq = n_batch // n_steps
    rows = n_seq * t_new
    blk0 = row0 // rows
    cache_spec = pl.BlockSpec((n_seq, A_KV_WIDTH, WINDOW), lambda s: (s, 0, 0))

    def fn(step, in_refs, out_refs):
        return _swa_sample_block(*in_refs, *out_refs, n_seq=n_seq, t_new=t_new)

    return dict(fn=fn, args=[sinks, pa_qz, pa_qz, pa_kv, pa_kv, cache_k, cache_v],
                in_specs=[pl.BlockSpec(memory_space=pltpu.SMEM),
                          pl.BlockSpec((rows, A_WIDTH), lambda s: (blk0 + s, 0)),
                          pl.BlockSpec((rows, A_WIDTH), lambda s: (blk0 + s, 1)),
                          pl.BlockSpec((rows, A_KV_WIDTH), lambda s: (blk0 + s, 0)),
                          pl.BlockSpec((rows, A_KV_WIDTH), lambda s: (blk0 + s, 1)),
                          cache_spec, cache_spec],
                out_specs=[pl.BlockSpec((rows, A_WIDTH), lambda s: (s, 0)), cache_spec, cache_spec],
                out_shape=[jax.ShapeDtypeStruct((n_batch * t_new, A_WIDTH), BF16),
                           jax.ShapeDtypeStruct(cache_k.shape, F32),
                           jax.ShapeDtypeStruct(cache_v.shape, F32)])


def _merge_tile(mm_ref, ma_ref, wbm_ref, wba_ref, gm_ref, ga_ref, o_ref):
    bm = jnp.dot(mm_ref[...], wbm_ref[...], preferred_element_type=F32)
    yield
    ba = jnp.dot(ma_ref[...], wba_ref[...], preferred_element_type=F32)
    yield
    o_ref[...] = (gm_ref[...].astype(F32) * bm + ga_ref[...].astype(F32) * ba).astype(o_ref.dtype)


def _merge(mix_m, mix_a, w_bm, w_ba, gates, *, m_rows, tn, row0, side=None, name):
    tm = min(1024, m_rows)
    nj = D_MODEL // tn
    blk0 = row0 // tm

    def fn(step, in_refs, out_refs):
        return _merge_tile(*in_refs, *out_refs)

    main = dict(fn=fn, args=[mix_m, mix_a, w_bm, w_ba, gates, gates],
                in_specs=[pl.BlockSpec((tm, M_WIDTH), lambda s: (s // nj, 0)),
                          pl.BlockSpec((tm, A_WIDTH), lambda s: (s // nj, 0)),
                          pl.BlockSpec((M_WIDTH, tn), lambda s: (0, s % nj)),
                          pl.BlockSpec((A_WIDTH, tn), lambda s: (0, s % nj)),
                          pl.BlockSpec((tm, tn), lambda s: (blk0 + s // nj, s % nj)),
                          pl.BlockSpec((tm, tn), lambda s: (blk0 + s // nj, s % nj + nj))],
                out_specs=[pl.BlockSpec((tm, tn), lambda s: (s // nj, s % nj))],
                out_shape=[jax.ShapeDtypeStruct((m_rows, D_MODEL), BF16)])
    (out,), side_outs = _hosted_call((m_rows // tm) * nj, main, side, name)
    return out, side_outs


def _out_tile(j, mg_ref, w_ref, x_ref, g_ref, b_ref, o_ref, *, tn):
    nj = o_ref.shape[1] // tn
    o_ref[:, pl.ds(pl.multiple_of(j * tn, tn), tn)] = (
        DEEPNORM_ALPHA * x_ref[...] + jnp.dot(mg_ref[...], w_ref[...], preferred_element_type=F32))
    yield

    @pl.when(j == nj - 1)
    def _():
        chunk = 16 * SUBLANES

        def norm_rows(r, carry):
            rows = pl.ds(pl.multiple_of(r * chunk, chunk), chunk)
            y = o_ref[rows, :]
            yc = y - jnp.mean(y, axis=1, keepdims=True)
            var = jnp.mean(jnp.square(yc), axis=1, keepdims=True)
            o_ref[rows, :] = yc * lax.rsqrt(var + LN_EPS) * g_ref[...] + b_ref[...]
            return carry

        lax.fori_loop(0, o_ref.shape[0] // chunk, norm_rows, 0)


def _tail_fused_body(mm_ref, ma_ref, wbm_ref, wba_ref, gm_ref, ga_ref, wo_ref, x_ref, g_ref, b_ref, o_ref, mg_ref,
                     *, n_merge, tn_m, tn_o):
    j = pl.program_id(1)

    @pl.when(j < n_merge)
    def _():
        bm = jnp.dot(mm_ref[...], wbm_ref[...], preferred_element_type=F32)
        ba = jnp.dot(ma_ref[...], wba_ref[...], preferred_element_type=F32)
        mg_ref[:, pl.ds(pl.multiple_of(j * tn_m, tn_m), tn_m)] = (
            gm_ref[...].astype(F32) * bm + ga_ref[...].astype(F32) * ba).astype(mg_ref.dtype)

    @pl.when(j >= n_merge)
    def _():
        col = pl.multiple_of((j - n_merge) * tn_o, tn_o)
        o_ref[:, pl.ds(col, tn_o)] = (
            DEEPNORM_ALPHA * x_ref[...] + jnp.dot(mg_ref[...], wo_ref[...], preferred_element_type=F32))

    @pl.when(j == pl.num_programs(1) - 1)
    def _():
        chunk = 8 * SUBLANES

        def norm_rows(r, carry):
            rows = pl.ds(pl.multiple_of(r * chunk, chunk), chunk)
            y = o_ref[rows, :]
            yc = y - jnp.mean(y, axis=1, keepdims=True)
            var = jnp.mean(jnp.square(yc), axis=1, keepdims=True)
            o_ref[rows, :] = yc * lax.rsqrt(var + LN_EPS) * g_ref[...] + b_ref[...]
            return carry

        lax.fori_loop(0, o_ref.shape[0] // chunk, norm_rows, 0)


def _tail_fused(mix_m, mix_a, w_bm, w_ba, gates, w_out, x2d, ln_g, ln_b, *, row0, tm, tn_m, tn_o, name):
    m_rows = mix_m.shape[0]
    n_merge, n_out = D_MODEL // tn_m, D_MODEL // tn_o
    blk0 = row0 // tm

    def mcol(j):
        return jnp.minimum(j, n_merge - 1)

    def ocol(j):
        return jnp.maximum(j - n_merge, 0)

    return pl.pallas_call(
        functools.partial(_tail_fused_body, n_merge=n_merge, tn_m=tn_m, tn_o=tn_o),
        grid=(m_rows // tm, n_merge + n_out),
        in_specs=[pl.BlockSpec((tm, M_WIDTH), lambda i, j: (i, 0)),
                  pl.BlockSpec((tm, A_WIDTH), lambda i, j: (i, 0)),
                  pl.BlockSpec((M_WIDTH, tn_m), lambda i, j: (0, mcol(j))),
                  pl.BlockSpec((A_WIDTH, tn_m), lambda i, j: (0, mcol(j))),
                  pl.BlockSpec((tm, tn_m), lambda i, j: (blk0 + i, mcol(j))),
                  pl.BlockSpec((tm, tn_m), lambda i, j: (blk0 + i, mcol(j) + n_merge)),
                  pl.BlockSpec((D_MODEL, tn_o), lambda i, j: (0, ocol(j))),
                  pl.BlockSpec((tm, tn_o), lambda i, j: (i, ocol(j))),
                  pl.BlockSpec((1, D_MODEL), lambda i, j: (0, 0)),
                  pl.BlockSpec((1, D_MODEL), lambda i, j: (0, 0))],
        out_specs=pl.BlockSpec((tm, D_MODEL), lambda i, j: (i, 0)),
        out_shape=jax.ShapeDtypeStruct((m_rows, D_MODEL), F32),
        scratch_shapes=[pltpu.VMEM((tm, D_MODEL), BF16)],
        compiler_params=_params(("parallel", "arbitrary")),
        name=name,
    )(mix_m, mix_a, w_bm, w_ba, gates, gates, w_out, x2d, ln_g, ln_b)


def _out_proj(merged, w_out, x2d, ln_g, ln_b, *, tm, tn, side=None, name):
    m_rows = merged.shape[0]
    nj = D_MODEL // tn

    def fn(step, in_refs, out_refs):
        return _out_tile(step % nj, *in_refs, *out_refs, tn=tn)

    main = dict(fn=fn, args=[merged, w_out, x2d, ln_g, ln_b],
                in_specs=[pl.BlockSpec((tm, D_MODEL), lambda s: (s // nj, 0)),
                          pl.BlockSpec((D_MODEL, tn), lambda s: (0, s % nj)),
                          pl.BlockSpec((tm, tn), lambda s: (s // nj, s % nj)),
                          pl.BlockSpec((1, D_MODEL), lambda s: (0, 0)),
                          pl.BlockSpec((1, D_MODEL), lambda s: (0, 0))],
                out_specs=[pl.BlockSpec((tm, D_MODEL), lambda s: (s // nj, 0))],
                out_shape=[jax.ShapeDtypeStruct((m_rows, D_MODEL), F32)])
    (out,), side_outs = _hosted_call((m_rows // tm) * nj, main, side, name)
    return out, side_outs


def _rope_tables(positions):
    half = A_HEAD_DIM // 2
    lane = jnp.arange(LANES)
    inv = ROPE_THETA ** (-(lane % half).astype(F32) / half)
    ang = positions.astype(F32)[:, None] * inv[None, :]
    sign = jnp.where((lane % A_HEAD_DIM) < half, -1.0, 1.0).astype(F32)
    return jnp.cos(ang), jnp.sin(ang) * sign[None, :]


def kernel(x_prompt, x_sample, state_C, state_n, state_m, cache_k, cache_v, w_in, b_if, norm_m_g,
           attn_sinks, w_bm, w_ba, w_out, ln_g, ln_b):
    bp, seq, _ = x_prompt.shape
    bs, t_new, _ = x_sample.shape
    w_buf = cache_k.shape[2]
    assert DEPTH == 1 and w_buf == WINDOW and seq % WINDOW == 0
    n_p, n_s = bp * seq, bs * t_new

    wt = jnp.swapaxes(w_in[0], 0, 1)
    w_attn = _repack_attn(wt)
    bif = jnp.pad(b_if[0], (0, LANES - 2 * M_HEADS)).reshape(1, LANES)
    ng = norm_m_g[0].reshape(1, M_WIDTH)
    sinks = attn_sinks[0]
    lng, lnb = ln_g[0].reshape(1, D_MODEL), ln_b[0].reshape(1, D_MODEL)

    xp2 = x_prompt.reshape(n_p, D_MODEL)
    xs2 = x_sample.reshape(n_s, D_MODEL)
    cos_p, sin_p = _rope_tables(jnp.arange(seq))
    cos_s, sin_s = _rope_tables(PAST_LEN + jnp.arange(t_new))
    cos = jnp.concatenate([jnp.tile(cos_p, (bp, 1)), jnp.tile(cos_s, (bs, 1))], axis=0)
    sin = jnp.concatenate([jnp.tile(sin_p, (bp, 1)), jnp.tile(sin_s, (bs, 1))], axis=0)
    tn = PROJ_TN
    n_m = (n_p + n_s) // PROJ_TM
    blk_gate = MLSTM_COLS // tn
    blk_if = ATTN_COLS // LANES
    xb, pa_kv = _project_first(xp2, xs2, w_attn, cos, sin, tm=PROJ_TM // 2, tn=tn, name="proj_kv_cast")
    attn_tm = PROJ_TM // 2
    qz_tiles = 2 * A_WIDTH // tn
    repack_side = _repack_rest_side(wt, n_steps=qz_tiles * ((n_p + n_s) // attn_tm))
    if_side = _gate_proj_side(xb, w_attn, blk_if, n_steps=qz_tiles * ((n_p + n_s) // attn_tm))
    pa_qz, (w_rest, g_if) = _project(xb, w_attn, 2 * A_KV_WIDTH // tn, qz_tiles, tn=tn, tm=attn_tm,
                                     out_dtype=BF16, mode="attn", cos=cos, sin=sin,
                                     side=_join_sides(repack_side, if_side), name="proj_qz_repack")
    tn_m = MLSTM_TN
    swa_side = _swa_prompt_side(pa_qz, pa_kv, sinks, n_steps=(MLSTM_COLS // tn_m) * n_m, n_batch=bp, seq=seq)
    cast_side = _cast_rows_side([w_bm[0], w_ba[0], w_out[0]], n_slabs=64)
    pm, (mix_a_p, wbm, wba, wo) = _project(xb, w_rest, 0, MLSTM_COLS // tn_m, tn=tn_m, out_dtype=BF16, mode="mlstm",
                                           side=_join_sides(swa_side, cast_side), name="proj_mlstm_swa_p")
    mlstm_side = _mlstm_prompt_side(pm, g_if, bif, ng, n_steps=(GATE_COLS // tn) * n_m, seq=seq, chunk=128)
    gates, (mix_m_p, p_c, p_n, p_m) = _project(xb, w_rest, blk_gate, GATE_COLS // tn, tn=tn, out_dtype=BF16,
                                               mode="sigmoid", side=mlstm_side, name="proj_gate_mlstm_p")

    init = (state_C[0], state_n[0], jnp.broadcast_to(state_m[0][:, :, None], (bs, M_HEADS, LANES)))
    merge_tn, out_tm, out_tn = 512, 512, 1024
    mlstm_s_side = _mlstm_sample_side(pm, g_if, bif, ng, init, n_steps=(n_p // 1024) * (D_MODEL // merge_tn),
                                      n_batch=bs, t_new=t_new, row0=n_p)
    def window_minor(c):
        return jnp.transpose(c, (0, 2, 3, 1)).reshape(bs, A_KV_WIDTH, w_buf)

    def window_major(c):
        return jnp.transpose(c.reshape(bs, A_KV_HEADS, A_HEAD_DIM, w_buf), (0, 3, 1, 2))[None]

    swa_s_side = _swa_sample_side(pa_qz, pa_kv, window_minor(cache_k[0]), window_minor(cache_v[0]), sinks,
                                  n_steps=(n_p // out_tm) * (D_MODEL // out_tn), n_batch=bs, t_new=t_new, row0=n_p)

    merged, (mix_m_s, s_c, s_n, s_m) = _merge(mix_m_p, mix_a_p, wbm, wba, gates, m_rows=n_p, tn=merge_tn, row0=0,
                                              side=mlstm_s_side, name="merge_p_mlstm_s")
    y_p, (mix_a_s, s_k, s_v) = _out_proj(merged, wo, xp2, lng, lnb, tm=out_tm, tn=out_tn, side=swa_s_side,
                                         name="out_p_swa_s")
    y_p = y_p.reshape(bp, seq, D_MODEL)
    kv_last = jnp.stack([pa_kv[(b + 1) * seq - w_buf:(b + 1) * seq] for b in range(bp)]).astype(F32)
    p_k = kv_last[..., :A_KV_WIDTH].reshape(1, bp, w_buf, A_KV_HEADS, A_HEAD_DIM)
    p_v = kv_last[..., A_KV_WIDTH:].reshape(1, bp, w_buf, A_KV_HEADS, A_HEAD_DIM)
    p_c, p_n, p_m = p_c[:bp], p_n[:bp], p_m[:bp]

    y_s = _tail_fused(mix_m_s, mix_a_s, wbm, wba, gates, wo, xs2, lng, lnb, row0=n_p, tm=out_tm, tn_m=merge_tn,
                      tn_o=merge_tn, name="tail_s").reshape(bs, t_new, D_MODEL)

    def st(a, like):
        return a[None].astype(like.dtype)

    return (y_p, y_s,
            st(p_c, state_C), st(p_n, state_n), st(p_m[:, :, 0], state_m), p_k, p_v,
            st(s_c, state_C), st(s_n, state_n), st(s_m[:, :, 0], state_m),
            window_major(s_k), window_major(s_v))
```
